```python
import math
import jax, jax.numpy as jnp
from jax import lax
import numpy as np

D_MODEL = 4096
BATCH = 2
SEQ = 4096
DEPTH = 2

CTX_LEN = 256
GRID_W = 64
HEAD_DIM = 128
AXIS_DIM = HEAD_DIM // 2
ROPE_THETA = 10000.0
EPS = 1e-6
Q_BLOCK = 128
ATTN_SCALE = 1.0 / math.sqrt(HEAD_DIM)

A_WIDTH = D_MODEL // 2
A_HEADS = A_WIDTH // HEAD_DIM
A_KV_HEADS = A_HEADS // 4
A_GROUP = A_HEADS // A_KV_HEADS
B_WIDTH = D_MODEL // 4
CONV_K = 3
C_WIDTH = D_MODEL // 4
C_VDIM = 2 * HEAD_DIM
C_HEADS = C_WIDTH // C_VDIM
MIX_WIDTH = A_WIDTH + B_WIDTH + C_WIDTH

KA_COLS = A_KV_HEADS * HEAD_DIM
VA_COLS = A_KV_HEADS * HEAD_DIM
KC_COLS = C_HEADS * 2 * HEAD_DIM
VC_COLS = C_WIDTH
KV_COLS = KA_COLS + VA_COLS + KC_COLS + VC_COLS
QA_COLS = A_WIDTH
QC_COLS = C_HEADS * 2 * HEAD_DIM
REST_COLS = QA_COLS + QC_COLS + 3 * B_WIDTH + A_WIDTH + B_WIDTH + C_WIDTH
IN_COLS = KV_COLS + REST_COLS

kernel_name = "hybrid_gqa_shortconv_diffattn_prefix_dit"


def _split(u, sizes):
    idx, acc = [], 0
    for s in sizes[:-1]:
        acc += s
        idx.append(acc)
    return jnp.split(u, idx, axis=-1)


def _rms(x, g):
    xf = x.astype(jnp.float32)
    y = xf * lax.rsqrt(jnp.mean(xf * xf, axis=-1, keepdims=True) + EPS)
    return y.astype(x.dtype) * g


def _modulation(cvec, w_mod, b_mod):
    m = jax.nn.silu(cvec) @ w_mod + b_mod
    return jnp.split(m, 3, axis=-1)


def _axial_rope_tables(n):
    rows = n // GRID_W
    row = jnp.broadcast_to(jnp.arange(rows)[:, None], (rows, GRID_W)).reshape(-1)
    col = jnp.broadcast_to(jnp.arange(GRID_W)[None, :], (rows, GRID_W)).reshape(-1)
    inv = ROPE_THETA ** (-jnp.arange(0, AXIS_DIM, 2, dtype=jnp.float32) / AXIS_DIM)
    ar = row.astype(jnp.float32)[:, None] * inv
    ac = col.astype(jnp.float32)[:, None] * inv
    return (jnp.cos(ar), jnp.sin(ar), jnp.cos(ac), jnp.sin(ac))


def _rot(xp, cos, sin):
    x1, x2 = jnp.split(xp, 2, axis=-1)
    return jnp.concatenate([x1 * cos - x2 * sin, x2 * cos + x1 * sin], axis=-1)


def _apply_rope(x, tabs):
    shp = (x.shape[1],) + (1,) * (x.ndim - 3) + (AXIS_DIM // 2,)
    cr, sr, cc, sc = [t.reshape(shp).astype(x.dtype) for t in tabs]
    xr, xc = x[..., :AXIS_DIM], x[..., AXIS_DIM:]
    return jnp.concatenate([_rot(xr, cr, sr), _rot(xc, cc, sc)], axis=-1)


def _conv3(x, w):
    xp = jnp.pad(x, ((0, 0), (1, 1), (0, 0)))
    return xp[:, :-2] * w[0] + xp[:, 1:-1] * w[1] + xp[:, 2:] * w[2]


def _attend_gqa(q, k, v):
    s = jnp.einsum('bqkgd,bskd->bkgqs', q, k, preferred_element_type=jnp.float32) * ATTN_SCALE
    p = jax.nn.softmax(s, axis=-1).astype(v.dtype)
    return jnp.einsum('bkgqs,bskd->bqkgd', p, v)


def _attend_diff(q, k, v, lam):
    s = jnp.einsum('bqhcd,bshcd->bhcqs', q, k, preferred_element_type=jnp.float32) * ATTN_SCALE
    p = jax.nn.softmax(s, axis=-1)
    pd = (p[:, :, 0] - lam.astype(jnp.float32) * p[:, :, 1]).astype(v.dtype)
    return jnp.einsum('bhqs,bshe->bqhe', pd, v)


def _sweep_blocks(fn, q):
    b, s = q.shape[:2]
    nb = s // Q_BLOCK
    qb = jnp.moveaxis(q.reshape((b, nb, Q_BLOCK) + q.shape[2:]), 1, 0)
    ob = lax.map(fn, qb)
    return jnp.moveaxis(ob, 0, 1).reshape((b, s) + ob.shape[3:])


def _prep_kv(ukv, k_norm_a, tabs):
    b, t = ukv.shape[:2]
    ka, va, kc, vc = _split(ukv, [KA_COLS, VA_COLS, KC_COLS, VC_COLS])
    ka = _rms(ka.reshape(b, t, A_KV_HEADS, HEAD_DIM), k_norm_a)
    va = va.reshape(b, t, A_KV_HEADS, HEAD_DIM)
    kc = kc.reshape(b, t, C_HEADS, 2, HEAD_DIM)
    vc = vc.reshape(b, t, C_HEADS, C_VDIM)
    if tabs is not None:
        ka = _apply_rope(ka, tabs)
        kc = _apply_rope(kc, tabs)
    return ka, va, kc, vc


def _mix(urest, ka, va, kc, vc, q_norm_a, conv_w, lam, subln_g, lambda_init, w_out, tabs):
    b, t = urest.shape[:2]
    qa, qc, xb, bb, cb, za, zb, zc = _split(
        urest, [QA_COLS, QC_COLS, B_WIDTH, B_WIDTH, B_WIDTH, A_WIDTH, B_WIDTH, C_WIDTH])
    qa = _rms(qa.reshape(b, t, A_KV_HEADS, A_GROUP, HEAD_DIM), q_norm_a)
    qc = qc.reshape(b, t, C_HEADS, 2, HEAD_DIM)
    fa = lambda qblk: _attend_gqa(qblk, ka, va)
    fc = lambda qblk: _attend_diff(qblk, kc, vc, lam)
    if tabs is not None:
        qa = _apply_rope(qa, tabs)
        qc = _apply_rope(qc, tabs)
        oa = _sweep_blocks(fa, qa)
        oc = _sweep_blocks(fc, qc)
    else:
        oa = fa(qa)
        oc = fc(qc)
    oa = oa.reshape(b, t, A_WIDTH) * jax.nn.silu(za)
    yb = bb * _conv3(cb * xb, conv_w) * jax.nn.silu(zb)
    oc = (_rms(oc, subln_g) * (1.0 - lambda_init)).reshape(b, t, C_WIDTH) * jax.nn.silu(zc)
    return jnp.concatenate([oa, yb, oc], axis=-1) @ w_out


def setup_inputs(seed: int = 0) -> dict:
    key = jax.random.key(seed)
    ks = jax.random.split(key, 20)
    f32 = jnp.float32
    nrm = lambda k, shp: jax.random.normal(k, shp, f32)
    return {
        "x": nrm(ks[0], (BATCH, SEQ, D_MODEL)),
        "c": nrm(ks[1], (BATCH, D_MODEL)),
        "ctx": nrm(ks[2], (BATCH, CTX_LEN, D_MODEL)),
        "c_ctx": nrm(ks[3], (D_MODEL,)),
        "w_mod": nrm(ks[4], (DEPTH, D_MODEL, 3 * D_MODEL)) * (0.5 * D_MODEL ** -0.5),
        "b_mod": nrm(ks[5], (DEPTH, 3 * D_MODEL)) * 0.01,
        "norm_g": 1.0 + 0.02 * nrm(ks[6], (DEPTH, D_MODEL)),
        "w_in": nrm(ks[7], (DEPTH, D_MODEL, IN_COLS)) * D_MODEL ** -0.5,
        "q_norm_a": 1.0 + 0.02 * nrm(ks[8], (DEPTH, HEAD_DIM)),
        "k_norm_a": 1.0 + 0.02 * nrm(ks[9], (DEPTH, HEAD_DIM)),
        "conv_w": nrm(ks[10], (DEPTH, CONV_K, B_WIDTH)) * CONV_K ** -0.5,
        "lambda_q1": 0.1 * nrm(ks[11], (DEPTH, HEAD_DIM)),
        "lambda_k1": 0.1 * nrm(ks[12], (DEPTH, HEAD_DIM)),
        "lambda_q2": 0.1 * nrm(ks[13], (DEPTH, HEAD_DIM)),
        "lambda_k2": 0.1 * nrm(ks[14], (DEPTH, HEAD_DIM)),
        "subln_g": 1.0 + 0.02 * nrm(ks[15], (DEPTH, C_VDIM)),
        "w_out": nrm(ks[16], (DEPTH, MIX_WIDTH, D_MODEL)) * MIX_WIDTH ** -0.5,
        "final_g": 1.0 + 0.02 * nrm(ks[17], (D_MODEL,)),
    }


def reference(x, c, ctx, c_ctx, w_mod, b_mod, norm_g, w_in, q_norm_a, k_norm_a, conv_w,
              lambda_q1, lambda_k1, lambda_q2, lambda_k2, subln_g, w_out, final_g):
    tabs = _axial_rope_tables(x.shape[1])
    h, hc = x, ctx
    for i in range(DEPTH):
        update_ctx = i < DEPTH - 1
        lambda_init = 0.8 - 0.6 * math.exp(-0.3 * i)
        lam = (jnp.exp(jnp.sum(lambda_q1[i] * lambda_k1[i])) -
               jnp.exp(jnp.sum(lambda_q2[i] * lambda_k2[i])) + lambda_init)
        shift, scale, gate = [m[:, None, :] for m in _modulation(c, w_mod[i], b_mod[i])]
        cshift, cscale, cgate = _modulation(c_ctx, w_mod[i], b_mod[i])
        n = _rms(h, norm_g[i]) * (1.0 + scale) + shift
        nc = _rms(hc, norm_g[i]) * (1.0 + cscale) + cshift
        u = n @ w_in[i]
        uc = nc @ (w_in[i] if update_ctx else w_in[i][:, :KV_COLS])
        kvl = _prep_kv(u[..., :KV_COLS], k_norm_a[i], tabs)
        kvc = _prep_kv(uc[..., :KV_COLS], k_norm_a[i], None)
        ka, va, kc, vc = [jnp.concatenate([a, b_], axis=1) for a, b_ in zip(kvc, kvl)]
        out = _mix(u[..., KV_COLS:], ka, va, kc, vc, q_norm_a[i], conv_w[i], lam, subln_g[i],
                   lambda_init, w_out[i], tabs)
        if update_ctx:
            out_c = _mix(uc[..., KV_COLS:], *kvc, q_norm_a[i], conv_w[i], lam, subln_g[i],
                         lambda_init, w_out[i], None)
            hc = hc + cgate * out_c
        h = h + gate * out
    return _rms(h, final_g)
```

```python
import functools
import math

import jax
import jax.numpy as jnp
from jax import lax
from jax.experimental import pallas as pl
from jax.experimental.pallas import tpu as pltpu

GRID_W = 64
HEAD_DIM = 128
AXIS_DIM = HEAD_DIM // 2
ROPE_THETA = 10000.0
EPS = 1e-6
ATTN_SCALE = 1.0 / math.sqrt(HEAD_DIM)
A_GROUP = 4
C_VDIM = 2 * HEAD_DIM

F32 = jnp.float32
BF16 = jnp.bfloat16

VMEM_LIMIT_BYTES = 56 * 1024 * 1024
ROW_TILE = 256
KV_CHUNK = 256
PROJ_TN = 512


def _cparams(sem):
    return pltpu.CompilerParams(dimension_semantics=sem, vmem_limit_bytes=VMEM_LIMIT_BYTES)


def _silu(x):
    return x * (1.0 / (1.0 + jnp.exp(-x)))


def _mod_kernel(c_ref, w_ref, b_ref, o_ref):
    s = _silu(c_ref[...]).astype(BF16)
    w = w_ref[0].astype(BF16)
    o_ref[0] = jnp.dot(s, w, preferred_element_type=F32) + b_ref[0]


def _modulation(cvecs, w_mod, b_mod):
    depth, d, n3 = w_mod.shape
    tn = PROJ_TN
    return pl.pallas_call(
        _mod_kernel,
        grid=(depth, n3 // tn),
        in_specs=[
            pl.BlockSpec((8, d), lambda l, j: (0, 0)),
            pl.BlockSpec((1, d, tn), lambda l, j: (l, 0, j)),
            pl.BlockSpec((1, 1, tn), lambda l, j: (l, 0, j)),
        ],
        out_specs=pl.BlockSpec((1, 8, tn), lambda l, j: (l, 0, j)),
        out_shape=jax.ShapeDtypeStruct((depth, 8, n3), F32),
        compiler_params=_cparams(("arbitrary", "arbitrary")),
        name="modulation",
    )(cvecs, w_mod, b_mod.reshape(depth, 1, n3))


def _norm_math(h, g, scale, shift):
    ms = jnp.mean(h * h, axis=-1, keepdims=True)
    return ((h * lax.rsqrt(ms + EPS)) * g * (1.0 + scale) + shift).astype(BF16)


def _norm0_kernel(tiles_per_seg, x_ref, c_ref, g_ref, sc_ref, sh_ref, n_ref, h_ref):
    is_ctx = (pl.program_id(0) % tiles_per_seg) == 0
    h = jnp.where(is_ctx, c_ref[...], x_ref[...])
    h_ref[...] = h
    n_ref[...] = _norm_math(h, g_ref[...], sc_ref[0], sh_ref[0])


def _norm1_kernel(h_ref, g_ref, sc_ref, sh_ref, n_ref):
    n_ref[...] = _norm_math(h_ref[...], g_ref[...], sc_ref[0], sh_ref[0])


def _mod_row(t, tiles_per_seg):
    return jnp.where(t % tiles_per_seg == 0, 2, t // tiles_per_seg)


def _norm_layer0(x2, c2, g, scale, shift, tiles_per_seg):
    d = x2.shape[1]
    lat_tiles = tiles_per_seg - 1
    n_tiles = (x2.shape[0] + c2.shape[0]) // ROW_TILE
    rows = n_tiles * ROW_TILE
    xmap = lambda t: ((t // tiles_per_seg) * lat_tiles + jnp.maximum(t % tiles_per_seg - 1, 0), 0)
    mrow = lambda t: (_mod_row(t, tiles_per_seg), 0, 0)
    return pl.pallas_call(
        functools.partial(_norm0_kernel, tiles_per_seg),
        grid=(n_tiles,),
        in_specs=[
            pl.BlockSpec((ROW_TILE, d), xmap),
            pl.BlockSpec((ROW_TILE, d), lambda t: (t // tiles_per_seg, 0)),
            pl.BlockSpec((1, d), lambda t: (0, 0)),
            pl.BlockSpec((1, 1, d), mrow),
            pl.BlockSpec((1, 1, d), mrow),
        ],
        out_specs=[pl.BlockSpec((ROW_TILE, d), lambda t: (t, 0)),
                   pl.BlockSpec((ROW_TILE, d), lambda t: (t, 0))],
        out_shape=[jax.ShapeDtypeStruct((rows, d), BF16), jax.ShapeDtypeStruct((rows, d), F32)],
        compiler_params=_cparams(("arbitrary",)),
        name="norm_layer0",
    )(x2, c2, g, scale, shift)


def _norm_layer1(h, g, scale, shift, tiles_per_seg):
    rows, d = h.shape
    mrow = lambda t: (_mod_row(t, tiles_per_seg), 0, 0)
    return pl.pallas_call(
        _norm1_kernel,
        grid=(rows // ROW_TILE,),
        in_specs=[
            pl.BlockSpec((ROW_TILE, d), lambda t: (t, 0)),
            pl.BlockSpec((1, d), lambda t: (0, 0)),
            pl.BlockSpec((1, 1, d), mrow),
            pl.BlockSpec((1, 1, d), mrow),
        ],
        out_specs=pl.BlockSpec((ROW_TILE, d), lambda t: (t, 0)),
        out_shape=jax.ShapeDtypeStruct((rows, d), BF16),
        compiler_params=_cparams(("arbitrary",)),
        name="norm_layer1",
    )(h, g, scale, shift)


def _rope_partner(x):
    lane = lax.broadcasted_iota(jnp.int32, x.shape, 1)
    fwd = pltpu.roll(x, HEAD_DIM - AXIS_DIM // 2, axis=1)
    bwd = pltpu.roll(x, AXIS_DIM // 2, axis=1)
    return jnp.where((lane % AXIS_DIM) < AXIS_DIM // 2, fwd, bwd)


def _proj_kernel(col_groups, n_ref, w_ref, cos_ref, sin_ref, qg_ref, kg_ref, o_ref, wbf_ref):
    ka_end, va_end, kc_end, vc_end, qa_end, qc_end = col_groups
    j = pl.program_id(0)

    @pl.when(pl.program_id(1) == 0)
    def _():
        wbf_ref[...] = w_ref[...].astype(BF16)

    acc = jnp.dot(n_ref[...], wbf_ref[...], preferred_element_type=F32)
    heads = acc.shape[1] // HEAD_DIM
    is_ka = j < ka_end
    is_qa = (j >= vc_end) & (j < qa_end)
    is_kc = (j >= va_end) & (j < kc_end)
    is_qc = (j >= qa_end) & (j < qc_end)
    is_q = j >= vc_end
    qscale = jnp.where(is_q, ATTN_SCALE, 1.0).astype(F32)

    @pl.when(is_ka | is_qa)
    def _():
        g = jnp.where(is_ka, kg_ref[...], qg_ref[...])
        cs = cos_ref[...] * qscale
        ss = sin_ref[...] * qscale
        for hd in range(heads):
            x = acc[:, hd * HEAD_DIM:(hd + 1) * HEAD_DIM]
            r = lax.rsqrt(jnp.mean(x * x, axis=-1, keepdims=True) + EPS)
            xg = x * g
            y = (xg * cs + _rope_partner(xg) * ss) * r
            o_ref[:, hd * HEAD_DIM:(hd + 1) * HEAD_DIM] = y.astype(BF16)

    @pl.when(is_kc | is_qc)
    def _():
        cs = cos_ref[...] * qscale
        ss = sin_ref[...] * qscale
        for hd in range(heads):
            x = acc[:, hd * HEAD_DIM:(hd + 1) * HEAD_DIM]
            y = x * cs + _rope_partner(x) * ss
            o_ref[:, hd * HEAD_DIM:(hd + 1) * HEAD_DIM] = y.astype(BF16)

    @pl.when(jnp.logical_not(is_ka | is_qa | is_kc | is_qc))
    def _():
        o_ref[...] = acc.astype(BF16)


def _in_projection(n, w_in, layer, cos, sin, qg, kg, col_groups, tm):
    rows, d = n.shape
    cols = w_in.shape[2]
    tn = PROJ_TN
    return pl.pallas_call(
        functools.partial(_proj_kernel, col_groups),
        grid=(cols // tn, rows // tm),
        in_specs=[
            pl.BlockSpec((tm, d), lambda j, i: (i, 0)),
            pl.BlockSpec((None, d, tn), lambda j, i: (layer, 0, j)),
            pl.BlockSpec((tm, HEAD_DIM), lambda j, i: (i, 0)),
            pl.BlockSpec((tm, HEAD_DIM), lambda j, i: (i, 0)),
            pl.BlockSpec((1, HEAD_DIM), lambda j, i: (0, 0)),
            pl.BlockSpec((1, HEAD_DIM), lambda j, i: (0, 0)),
        ],
        out_specs=pl.BlockSpec((tm, tn), lambda j, i: (i, j)),
        out_shape=jax.ShapeDtypeStruct((rows, cols), BF16),
        scratch_shapes=[pltpu.VMEM((d, tn), BF16)],
        compiler_params=_cparams(("arbitrary", "arbitrary")),
        name="in_projection",
    )(n, w_in, cos, sin, qg, kg)


def _attn_a_kernel(n_chunks, q_ref, k_ref, v_ref, z_ref, o_ref, m_ref, l_ref, acc_ref):
    tq = q_ref.shape[0]
    q = q_ref[...]
    q4 = jnp.concatenate([q[:, g * HEAD_DIM:(g + 1) * HEAD_DIM] for g in range(A_GROUP)], axis=0)
    m_ref[...] = jnp.full(m_ref.shape, -jnp.inf, F32)
    l_ref[...] = jnp.zeros(l_ref.shape, F32)
    acc_ref[...] = jnp.zeros(acc_ref.shape, F32)
    nk = jnp.where(pl.program_id(2) == 0, 1, n_chunks)

    def body(c, carry):
        off = pl.multiple_of(c * KV_CHUNK, KV_CHUNK)
        k = k_ref[pl.ds(off, KV_CHUNK), :]
        v = v_ref[pl.ds(off, KV_CHUNK), :]
        s = lax.dot_general(q4, k, (((1,), (1,)), ((), ())), preferred_element_type=F32)
        m_old = m_ref[...]
        m_new = jnp.maximum(m_old, jnp.max(s, axis=-1, keepdims=True))
        alpha = jnp.exp(m_old - m_new)
        p = jnp.exp(s - m_new)
        l_ref[...] = alpha * l_ref[...] + jnp.sum(p, axis=-1, keepdims=True)
        acc_ref[...] = alpha * acc_ref[...] + jnp.dot(p.astype(BF16), v, preferred_element_type=F32)
        m_ref[...] = m_new
        return carry

    lax.fori_loop(0, nk, body, 0)
    o = acc_ref[...] * (1.0 / l_ref[...])
    z = z_ref[...].astype(F32)
    for g in range(A_GROUP):
        og = o[g * tq:(g + 1) * tq] * _silu(z[:, g * HEAD_DIM:(g + 1) * HEAD_DIM])
        o_ref[:, g * HEAD_DIM:(g + 1) * HEAD_DIM] = og.astype(BF16)


def _attention_a(u, batch, seg, cols):
    rows = u.shape[0]
    ka0, va0, qa0, za0, a_width = cols
    kv_heads = a_width // (A_GROUP * HEAD_DIM)
    tq = ROW_TILE
    qw = A_GROUP * HEAD_DIM
    tiles = seg // tq
    return pl.pallas_call(
        functools.partial(_attn_a_kernel, seg // KV_CHUNK),
        grid=(batch, kv_heads, tiles),
        in_specs=[
            pl.BlockSpec((tq, qw), lambda b, h, t: (b * tiles + t, qa0 // qw + h)),
            pl.BlockSpec((seg, HEAD_DIM), lambda b, h, t: (b, ka0 // HEAD_DIM + h)),
            pl.BlockSpec((seg, HEAD_DIM), lambda b, h, t: (b, va0 // HEAD_DIM + h)),
            pl.BlockSpec((tq, qw), lambda b, h, t: (b * tiles + t, za0 // qw + h)),
        ],
        out_specs=pl.BlockSpec((tq, qw), lambda b, h, t: (b * tiles + t, h)),
        out_shape=jax.ShapeDtypeStruct((rows, a_width), BF16),
        scratch_shapes=[pltpu.VMEM((A_GROUP * tq, 1), F32), pltpu.VMEM((A_GROUP * tq, 1), F32),
                        pltpu.VMEM((A_GROUP * tq, HEAD_DIM), F32)],
        compiler_params=_cparams(("arbitrary", "arbitrary", "arbitrary")),
        name="attention_a",
    )(u, u, u, u)


def _attn_c_kernel(n_chunks, one_minus_lambda_init, lambda_init,
                   q_ref, k_ref, v_ref, z_ref, lq1_ref, lk1_ref, lq2_ref, lk2_ref, sg_ref,
                   o_ref, m_ref, l_ref, acc_ref):
    tq = q_ref.shape[0]
    q = q_ref[...]
    q2 = jnp.concatenate([q[:, :HEAD_DIM], q[:, HEAD_DIM:]], axis=0)
    m_ref[...] = jnp.full(m_ref.shape, -jnp.inf, F32)
    l_ref[...] = jnp.zeros(l_ref.shape, F32)
    acc_ref[...] = jnp.zeros(acc_ref.shape, F32)
    nk = jnp.where(pl.program_id(2) == 0, 1, n_chunks)
    nt = (((1,), (1,)), ((), ()))

    def body(c, carry):
        off = pl.multiple_of(c * KV_CHUNK, KV_CHUNK)
        k = k_ref[pl.ds(off, KV_CHUNK), :]
        v = v_ref[pl.ds(off, KV_CHUNK), :]
        s0 = lax.dot_general(q2[:tq], k[:, :HEAD_DIM], nt, preferred_element_type=F32)
        s1 = lax.dot_general(q2[tq:], k[:, HEAD_DIM:], nt, preferred_element_type=F32)
        s = jnp.concatenate([s0, s1], axis=0)
        m_old = m_ref[...]
        m_new = jnp.maximum(m_old, jnp.max(s, axis=-1, keepdims=True))
        alpha = jnp.exp(m_old - m_new)
        p = jnp.exp(s - m_new)
        l_ref[...] = alpha * l_ref[...] + jnp.sum(p, axis=-1, keepdims=True)
        acc_ref[...] = alpha * acc_ref[...] + jnp.dot(p.astype(BF16), v, preferred_element_type=F32)
        m_ref[...] = m_new
        return carry

    lax.fori_loop(0, nk, body, 0)
    lam = (jnp.exp(jnp.sum(lq1_ref[...] * lk1_ref[...], axis=-1, keepdims=True)) -
           jnp.exp(jnp.sum(lq2_ref[...] * lk2_ref[...], axis=-1, keepdims=True)) + lambda_init)
    o = acc_ref[...] * (1.0 / l_ref[...])
    od = o[:tq] - lam * o[tq:]
    y = od * lax.rsqrt(jnp.mean(od * od, axis=-1, keepdims=True) + EPS) * sg_ref[...]
    y = y * one_minus_lambda_init * _silu(z_ref[...].astype(F32))
    o_ref[...] = y.astype(BF16)


def _attention_c(u, lam_vecs, subln_g, lambda_init, batch, seg, cols):
    rows = u.shape[0]
    kc0, vc0, qc0, zc0, c_width = cols
    heads = c_width // C_VDIM
    tq = ROW_TILE
    tiles = seg // tq
    vec = pl.BlockSpec((1, HEAD_DIM), lambda b, h, t: (0, 0))
    return pl.pallas_call(
        functools.partial(_attn_c_kernel, seg // KV_CHUNK, 1.0 - lambda_init, lambda_init),
        grid=(batch, heads, tiles),
        in_specs=[
            pl.BlockSpec((tq, C_VDIM), lambda b, h, t: (b * tiles + t, qc0 // C_VDIM + h)),
            pl.BlockSpec((seg, C_VDIM), lambda b, h, t: (b, kc0 // C_VDIM + h)),
            pl.BlockSpec((seg, C_VDIM), lambda b, h, t: (b, vc0 // C_VDIM + h)),
            pl.BlockSpec((tq, C_VDIM), lambda b, h, t: (b * tiles + t, zc0 // C_VDIM + h)),
            vec, vec, vec, vec,
            pl.BlockSpec((1, C_VDIM), lambda b, h, t: (0, 0)),
        ],
        out_specs=pl.BlockSpec((tq, C_VDIM), lambda b, h, t: (b * tiles + t, h)),
        out_shape=jax.ShapeDtypeStruct((rows, c_width), BF16),
        scratch_shapes=[pltpu.VMEM((2 * tq, 1), F32), pltpu.VMEM((2 * tq, 1), F32),
                        pltpu.VMEM((2 * tq, C_VDIM), F32)],
        compiler_params=_cparams(("arbitrary", "arbitrary", "arbitrary")),
        name="attention_c",
    )(u, u, u, u, *lam_vecs, subln_g)


HALO = 16


def _conv_kernel(tiles_per_seg, xb_ref, bb_ref, cb_ref, zb_ref, xp_ref, cp_ref, xn_ref, cn_ref,
                 w_ref, o_ref, g_ref):
    tr = xb_ref.shape[0]
    r = pl.program_id(0) % tiles_per_seg
    has_prev = ((r != 0) & (r != 1)).astype(F32)
    has_next = ((r != 0) & (r != tiles_per_seg - 1)).astype(F32)
    g = cb_ref[...].astype(F32) * xb_ref[...].astype(F32)
    g_ref[0:HALO, :] = cp_ref[...].astype(F32) * xp_ref[...].astype(F32) * has_prev
    g_ref[HALO:HALO + tr, :] = g
    g_ref[HALO + tr:HALO + tr + HALO, :] = cn_ref[...].astype(F32) * xn_ref[...].astype(F32) * has_next
    w = w_ref[...]
    conv = (g_ref[HALO - 1:HALO - 1 + tr, :] * w[0:1] + g * w[1:2] +
            g_ref[HALO + 1:HALO + 1 + tr, :] * w[2:3])
    y = bb_ref[...].astype(F32) * conv * _silu(zb_ref[...].astype(F32))
    o_ref[...] = y.astype(BF16)


def _short_conv(u, conv_w, tiles_per_seg, cols):
    rows = u.shape[0]
    xb0, bb0, cb0, zb0, bw = cols
    tr = ROW_TILE
    n_tiles = rows // tr
    hpt = tr // HALO
    last_halo = rows // HALO - 1
    main = lambda c0: pl.BlockSpec((tr, bw), lambda t: (t, c0 // bw))
    prev = lambda c0: pl.BlockSpec((HALO, bw), lambda t: (jnp.maximum(t * hpt - 1, 0), c0 // bw))
    nxt = lambda c0: pl.BlockSpec((HALO, bw), lambda t: (jnp.minimum((t + 1) * hpt, last_halo), c0 // bw))
    return pl.pallas_call(
        functools.partial(_conv_kernel, tiles_per_seg),
        grid=(n_tiles,),
        in_specs=[main(xb0), main(bb0), main(cb0), main(zb0),
                  prev(xb0), prev(cb0), nxt(xb0), nxt(cb0),
                  pl.BlockSpec(conv_w.shape, lambda t: (0, 0))],
        out_specs=pl.BlockSpec((tr, bw), lambda t: (t, 0)),
        out_shape=jax.ShapeDtypeStruct((rows, bw), BF16),
        scratch_shapes=[pltpu.VMEM((tr + 2 * HALO, bw), F32)],
        compiler_params=_cparams(("arbitrary",)),
        name="short_conv",
    )(u, u, u, u, u, u, u, u, conv_w)


def _out_kernel(seg, ctx_len, oa_ref, yb_ref, oc_ref, w_ref, gate_ref, h_ref, o_ref, wbf_ref):
    @pl.when(pl.program_id(1) == 0)
    def _():
        wbf_ref[...] = w_ref[...].astype(BF16)

    ka = oa_ref.shape[1]
    kb = yb_ref.shape[1]
    acc = jnp.dot(oa_ref[...], wbf_ref[0:ka, :], preferred_element_type=F32)
    acc += jnp.dot(yb_ref[...], wbf_ref[ka:ka + kb, :], preferred_element_type=F32)
    acc += jnp.dot(oc_ref[...], wbf_ref[ka + kb:, :], preferred_element_type=F32)
    tm = acc.shape[0]
    row = pl.program_id(1) * tm + lax.broadcasted_iota(jnp.int32, (tm, 1), 0)
    is_ctx = (row % seg) < ctx_len
    gate = jnp.where(is_ctx, gate_ref[2:3, :], jnp.where(row < seg, gate_ref[0:1, :], gate_ref[1:2, :]))
    o_ref[...] = h_ref[...] + gate * acc


def _out_projection(oa, yb, oc, w_out, layer, gates, h, seg, ctx_len, tm):
    rows, d = h.shape
    tn = PROJ_TN
    kdim = w_out.shape[1]
    return pl.pallas_call(
        functools.partial(_out_kernel, seg, ctx_len),
        grid=(d // tn, rows // tm),
        in_specs=[
            pl.BlockSpec((tm, oa.shape[1]), lambda j, i: (i, 0)),
            pl.BlockSpec((tm, yb.shape[1]), lambda j, i: (i, 0)),
            pl.BlockSpec((tm, oc.shape[1]), lambda j, i: (i, 0)),
            pl.BlockSpec((None, kdim, tn), lambda j, i: (layer, 0, j)),
            pl.BlockSpec((8, tn), lambda j, i: (0, j)),
            pl.BlockSpec((tm, tn), lambda j, i: (i, j)),
        ],
        out_specs=pl.BlockSpec((tm, tn), lambda j, i: (i, j)),
        out_shape=jax.ShapeDtypeStruct((rows, d), F32),
        scratch_shapes=[pltpu.VMEM((kdim, tn), BF16)],
        compiler_params=_cparams(("arbitrary", "arbitrary")),
        name="out_projection",
    )(oa, yb, oc, w_out, gates, h)


def _final_kernel(h_ref, g_ref, o_ref):
    h = h_ref[...]
    ms = jnp.mean(h * h, axis=-1, keepdims=True)
    o_ref[...] = (h * lax.rsqrt(ms + EPS)) * g_ref[...]


def _final_norm(h, g, batch, tiles_per_seg):
    d = h.shape[1]
    lat_tiles = tiles_per_seg - 1
    return pl.pallas_call(
        _final_kernel,
        grid=(batch * lat_tiles,),
        in_specs=[
            pl.BlockSpec((ROW_TILE, d), lambda t: ((t // lat_tiles) * tiles_per_seg + 1 + t % lat_tiles, 0)),
            pl.BlockSpec((1, d), lambda t: (0, 0)),
        ],
        out_specs=pl.BlockSpec((ROW_TILE, d), lambda t: (t, 0)),
        out_shape=jax.ShapeDtypeStruct((batch * lat_tiles * ROW_TILE, d), F32),
        compiler_params=_cparams(("arbitrary",)),
        name="final_norm",
    )(h, g)


def _rope_tables(batch, seq, ctx_len):
    pos = jnp.arange(seq)
    inv = ROPE_THETA ** (-jnp.arange(0, AXIS_DIM, 2, dtype=F32) / AXIS_DIM)
    ar = (pos // GRID_W).astype(F32)[:, None] * inv
    ac = (pos % GRID_W).astype(F32)[:, None] * inv
    cos = jnp.concatenate([jnp.cos(ar), jnp.cos(ar), jnp.cos(ac), jnp.cos(ac)], axis=-1)
    sin = jnp.concatenate([-jnp.sin(ar), jnp.sin(ar), -jnp.sin(ac), jnp.sin(ac)], axis=-1)
    cos = jnp.concatenate([jnp.ones((ctx_len, HEAD_DIM), F32), cos], axis=0)
    sin = jnp.concatenate([jnp.zeros((ctx_len, HEAD_DIM), F32), sin], axis=0)
    return jnp.tile(cos, (batch, 1)), jnp.tile(sin, (batch, 1))


def _largest_tile(rows, cap):
    best = 16
    for t in range(16, cap + 1, 16):
        if rows % t == 0:
            best = t
    return best


def kernel(x, c, ctx, c_ctx, w_mod, b_mod, norm_g, w_in, q_norm_a, k_norm_a, conv_w, lambda_q1,
           lambda_k1, lambda_q2, lambda_k2, subln_g, w_out, final_g):
    batch, seq, d = x.shape
    ctx_len = ctx.shape[1]
    depth = w_in.shape[0]
    seg = ctx_len + seq
    tiles_per_seg = seg // ROW_TILE
    assert ctx_len == ROW_TILE == KV_CHUNK and seq % ROW_TILE == 0

    a_width, b_width, c_width = d // 2, d // 4, d // 4
    kv_a = a_width // A_GROUP
    ka0 = 0
    va0 = ka0 + kv_a
    kc0 = va0 + kv_a
    vc0 = kc0 + c_width
    qa0 = vc0 + c_width
    qc0 = qa0 + a_width
    xb0 = qc0 + c_width
    bb0 = xb0 + b_width
    cb0 = bb0 + b_width
    za0 = cb0 + b_width
    zb0 = za0 + a_width
    zc0 = zb0 + b_width
    assert zc0 + c_width == w_in.shape[2]
    col_groups = tuple(v // PROJ_TN for v in (va0, kc0, vc0, qa0, qc0, xb0))

    cvecs = jnp.concatenate([c, c_ctx[None, :], jnp.zeros((8 - batch - 1, d), F32)], axis=0)
    mods = _modulation(cvecs, w_mod, b_mod)
    cos, sin = _rope_tables(batch, seq, ctx_len)
    rows = batch * seg
    tm = _largest_tile(rows, 1088)

    h = None
    for i in range(depth):
        lambda_init = 0.8 - 0.6 * math.exp(-0.3 * i)
        shift = mods[i, :, 0:d].reshape(8, 1, d)
        scale = mods[i, :, d:2 * d].reshape(8, 1, d)
        gates = mods[i, :, 2 * d:3 * d]
        g = norm_g[i].reshape(1, d)
        if i == 0:
            n, h = _norm_layer0(x.reshape(batch * seq, d), ctx.reshape(batch * ctx_len, d), g,
                                scale, shift, tiles_per_seg)
        else:
            n = _norm_layer1(h, g, scale, shift, tiles_per_seg)
        u = _in_projection(n, w_in, i, cos, sin, q_norm_a[i].reshape(1, HEAD_DIM),
                           k_norm_a[i].reshape(1, HEAD_DIM), col_groups, tm)
        oa = _attention_a(u, batch, seg, (ka0, va0, qa0, za0, a_width))
        lam_vecs = [v[i].reshape(1, HEAD_DIM) for v in (lambda_q1, lambda_k1, lambda_q2, lambda_k2)]
        oc = _attention_c(u, lam_vecs, subln_g[i].reshape(1, C_VDIM), lambda_init, batch, seg,
                          (kc0, vc0, qc0, zc0, c_width))
        yb = _short_conv(u, conv_w[i], tiles_per_seg, (xb0, bb0, cb0, zb0, b_width))
        h = _out_projection(oa, yb, oc, w_out, i, gates, h, seg, ctx_len, tm)
    out = _final_norm(h, final_g.reshape(1, d), batch, tiles_per_seg)
    return out.reshape(batch, seq, d)
```

```python
import functools
import math

import jax
import jax.numpy as jnp
from jax import lax
from jax.experimental import pallas as pl
from jax.experimental.pallas import tpu as pltpu

GRID_W = 64
HEAD_DIM = 128
AXIS_DIM = HEAD_DIM // 2
ROPE_THETA = 10000.0
EPS = 1e-6
ATTN_SCALE = 1.0 / math.sqrt(HEAD_DIM)
Q_PRESCALE = ATTN_SCALE * math.log2(math.e)
A_GROUP = 4
C_VDIM = 2 * HEAD_DIM

F32 = jnp.float32
BF16 = jnp.bfloat16

VMEM_LIMIT_BYTES = 56 * 1024 * 1024
ROW_TILE = 256
KV_CHUNK = 256
LAT_CHUNK = 512
PROJ_TN = 512


def _cparams(sem):
    return pltpu.CompilerParams(dimension_semantics=sem, vmem_limit_bytes=VMEM_LIMIT_BYTES)


def _silu(x):
    return x * (1.0 / (1.0 + jnp.exp(-x)))


def _mod_kernel(c_ref, w_ref, b_ref, o_ref):
    s = _silu(c_ref[...]).astype(BF16)
    w = w_ref[0].astype(BF16)
    o_ref[0] = jnp.dot(s, w, preferred_element_type=F32) + b_ref[0]


def _modulation(cvecs, w_mod, b_mod):
    depth, d, n3 = w_mod.shape
    tn = PROJ_TN
    return pl.pallas_call(
        _mod_kernel,
        grid=(depth, n3 // tn),
        in_specs=[
            pl.BlockSpec((8, d), lambda l, j: (0, 0)),
            pl.BlockSpec((1, d, tn), lambda l, j: (l, 0, j)),
            pl.BlockSpec((1, 1, tn), lambda l, j: (l, 0, j)),
        ],
        out_specs=pl.BlockSpec((1, 8, tn), lambda l, j: (l, 0, j)),
        out_shape=jax.ShapeDtypeStruct((depth, 8, n3), F32),
        compiler_params=_cparams(("arbitrary", "arbitrary")),
        name="modulation",
    )(cvecs, w_mod, b_mod.reshape(depth, 1, n3))


def _norm_math(h, g, scale, shift):
    ms = jnp.mean(h * h, axis=-1, keepdims=True)
    return ((h * lax.rsqrt(ms + EPS)) * g * (1.0 + scale) + shift).astype(BF16)


def _norm0_kernel(tiles_per_seg, x_ref, c_ref, g_ref, sc_ref, sh_ref, n_ref, h_ref):
    is_ctx = (pl.program_id(0) % tiles_per_seg) == 0
    h = jnp.where(is_ctx, c_ref[...], x_ref[...])
    h_ref[...] = h
    n_ref[...] = _norm_math(h, g_ref[...], sc_ref[0], sh_ref[0])


def _norm1_kernel(h_ref, g_ref, sc_ref, sh_ref, n_ref):
    n_ref[...] = _norm_math(h_ref[...], g_ref[...], sc_ref[0], sh_ref[0])


def _mod_row(t, tiles_per_seg):
    return jnp.where(t % tiles_per_seg == 0, 2, t // tiles_per_seg)


def _norm_layer0(x2, c2, g, scale, shift, tiles_per_seg):
    d = x2.shape[1]
    lat_tiles = tiles_per_seg - 1
    n_tiles = (x2.shape[0] + c2.shape[0]) // ROW_TILE
    rows = n_tiles * ROW_TILE
    xmap = lambda t: ((t // tiles_per_seg) * lat_tiles + jnp.maximum(t % tiles_per_seg - 1, 0), 0)
    mrow = lambda t: (_mod_row(t, tiles_per_seg), 0, 0)
    return pl.pallas_call(
        functools.partial(_norm0_kernel, tiles_per_seg),
        grid=(n_tiles,),
        in_specs=[
            pl.BlockSpec((ROW_TILE, d), xmap),
            pl.BlockSpec((ROW_TILE, d), lambda t: (t // tiles_per_seg, 0)),
            pl.BlockSpec((1, d), lambda t: (0, 0)),
            pl.BlockSpec((1, 1, d), mrow),
            pl.BlockSpec((1, 1, d), mrow),
        ],
        out_specs=[pl.BlockSpec((ROW_TILE, d), lambda t: (t, 0)),
                   pl.BlockSpec((ROW_TILE, d), lambda t: (t, 0))],
        out_shape=[jax.ShapeDtypeStruct((rows, d), BF16), jax.ShapeDtypeStruct((rows, d), F32)],
        compiler_params=_cparams(("arbitrary",)),
        name="norm_layer0",
    )(x2, c2, g, scale, shift)


def _norm_layer1(h, g, scale, shift, tiles_per_seg):
    rows, d = h.shape
    mrow = lambda t: (_mod_row(t, tiles_per_seg), 0, 0)
    return pl.pallas_call(
        _norm1_kernel,
        grid=(rows // ROW_TILE,),
        in_specs=[
            pl.BlockSpec((ROW_TILE, d), lambda t: (t, 0)),
            pl.BlockSpec((1, d), lambda t: (0, 0)),
            pl.BlockSpec((1, 1, d), mrow),
            pl.BlockSpec((1, 1, d), mrow),
        ],
        out_specs=pl.BlockSpec((ROW_TILE, d), lambda t: (t, 0)),
        out_shape=jax.ShapeDtypeStruct((rows, d), BF16),
        compiler_params=_cparams(("arbitrary",)),
        name="norm_layer1",
    )(h, g, scale, shift)


def _rope_partner(x):
    lane = lax.broadcasted_iota(jnp.int32, x.shape, 1)
    fwd = pltpu.roll(x, HEAD_DIM - AXIS_DIM // 2, axis=1)
    bwd = pltpu.roll(x, AXIS_DIM // 2, axis=1)
    return jnp.where((lane % AXIS_DIM) < AXIS_DIM // 2, fwd, bwd)


def _proj_kernel(col_groups, n_ref, w_ref, cos_ref, sin_ref, qg_ref, kg_ref, o_ref, wbf_ref):
    ka_end, va_end, kc_end, vc_end, qa_end, qc_end = col_groups
    j = pl.program_id(0)

    @pl.when(pl.program_id(1) == 0)
    def _():
        wbf_ref[...] = w_ref[...].astype(BF16)

    acc = jnp.dot(n_ref[...], wbf_ref[...], preferred_element_type=F32)
    heads = acc.shape[1] // HEAD_DIM
    is_ka = j < ka_end
    is_qa = (j >= vc_end) & (j < qa_end)
    is_kc = (j >= va_end) & (j < kc_end)
    is_qc = (j >= qa_end) & (j < qc_end)
    is_q = j >= vc_end
    qscale = jnp.where(is_q, Q_PRESCALE, 1.0).astype(F32)

    @pl.when(is_ka | is_qa)
    def _():
        g = jnp.where(is_ka, kg_ref[...], qg_ref[...])
        cs = cos_ref[...] * qscale
        ss = sin_ref[...] * qscale
        for hd in range(heads):
            x = acc[:, hd * HEAD_DIM:(hd + 1) * HEAD_DIM]
            r = lax.rsqrt(jnp.mean(x * x, axis=-1, keepdims=True) + EPS)
            xg = x * g
            y = (xg * cs + _rope_partner(xg) * ss) * r
            o_ref[:, hd * HEAD_DIM:(hd + 1) * HEAD_DIM] = y.astype(BF16)

    @pl.when(is_kc | is_qc)
    def _():
        cs = cos_ref[...] * qscale
        ss = sin_ref[...] * qscale
        for hd in range(heads):
            x = acc[:, hd * HEAD_DIM:(hd + 1) * HEAD_DIM]
            y = x * cs + _rope_partner(x) * ss
            o_ref[:, hd * HEAD_DIM:(hd + 1) * HEAD_DIM] = y.astype(BF16)

    @pl.when(jnp.logical_not(is_ka | is_qa | is_kc | is_qc))
    def _():
        o_ref[...] = acc.astype(BF16)


def _in_projection(n, w_in, layer, cos, sin, qg, kg, col_groups, tm):
    rows, d = n.shape
    cols = w_in.shape[2]
    tn = PROJ_TN
    return pl.pallas_call(
        functools.partial(_proj_kernel, col_groups),
        grid=(cols // tn, rows // tm),
        in_specs=[
            pl.BlockSpec((tm, d), lambda j, i: (i, 0)),
            pl.BlockSpec((None, d, tn), lambda j, i: (layer, 0, j)),
            pl.BlockSpec((tm, HEAD_DIM), lambda j, i: (i, 0)),
            pl.BlockSpec((tm, HEAD_DIM), lambda j, i: (i, 0)),
            pl.BlockSpec((1, HEAD_DIM), lambda j, i: (0, 0)),
            pl.BlockSpec((1, HEAD_DIM), lambda j, i: (0, 0)),
        ],
        out_specs=pl.BlockSpec((tm, tn), lambda j, i: (i, j)),
        out_shape=jax.ShapeDtypeStruct((rows, cols), BF16),
        scratch_shapes=[pltpu.VMEM((d, tn), BF16)],
        compiler_params=_cparams(("arbitrary", "arbitrary")),
        name="in_projection",
    )(n, w_in, cos, sin, qg, kg)


_NT = (((1,), (1,)), ((), ()))
_TN = (((0,), (0,)), ((), ()))


def _online_softmax(scores, v_ref, m_ref, l_ref, acc_ref, is_latent, n_lat_chunks):
    def chunk(off, size, first):
        st = scores(off, size)
        v = v_ref[pl.ds(off, size), :]
        m_new = jnp.max(st, axis=0, keepdims=True)
        if not first:
            m_old = m_ref[...]
            m_new = jnp.maximum(m_old, m_new)
            alpha = jnp.exp2(m_old - m_new)
        p = jnp.exp2(st - m_new)
        l_new = jnp.sum(p, axis=0, keepdims=True)
        acc_new = lax.dot_general(v, p.astype(BF16), _TN, preferred_element_type=F32)
        if not first:
            l_new += alpha * l_ref[...]
            acc_new += alpha * acc_ref[...]
        m_ref[...] = m_new
        l_ref[...] = l_new
        acc_ref[...] = acc_new

    chunk(0, KV_CHUNK, True)

    @pl.when(is_latent)
    def _():
        def body(c, carry):
            chunk(pl.multiple_of(KV_CHUNK + c * LAT_CHUNK, KV_CHUNK), LAT_CHUNK, False)
            return carry

        lax.fori_loop(0, n_lat_chunks, body, 0, unroll=True)


def _attn_a_kernel(n_chunks, q_ref, k_ref, v_ref, z_ref, o_ref, m_ref, l_ref, acc_ref):
    tq = q_ref.shape[0]
    q = q_ref[...]
    q4 = jnp.concatenate([q[:, g * HEAD_DIM:(g + 1) * HEAD_DIM] for g in range(A_GROUP)], axis=0)

    def scores(off, size):
        k = k_ref[pl.ds(off, size), :]
        return lax.dot_general(k, q4, _NT, preferred_element_type=F32)

    _online_softmax(scores, v_ref, m_ref, l_ref, acc_ref, pl.program_id(2) > 0, n_chunks)
    o = jnp.transpose(acc_ref[...] * (1.0 / l_ref[...]))
    z = z_ref[...].astype(F32)
    for g in range(A_GROUP):
        og = o[g * tq:(g + 1) * tq] * _silu(z[:, g * HEAD_DIM:(g + 1) * HEAD_DIM])
        o_ref[:, g * HEAD_DIM:(g + 1) * HEAD_DIM] = og.astype(BF16)


def _attention_a(u, batch, seg, cols):
    rows = u.shape[0]
    ka0, va0, qa0, za0, a_width = cols
    kv_heads = a_width // (A_GROUP * HEAD_DIM)
    tq = ROW_TILE
    qw = A_GROUP * HEAD_DIM
    tiles = seg // tq
    return pl.pallas_call(
        functools.partial(_attn_a_kernel, (seg - KV_CHUNK) // LAT_CHUNK),
        grid=(batch, kv_heads, tiles),
        in_specs=[
            pl.BlockSpec((tq, qw), lambda b, h, t: (b * tiles + t, qa0 // qw + h)),
            pl.BlockSpec((seg, HEAD_DIM), lambda b, h, t: (b, ka0 // HEAD_DIM + h)),
            pl.BlockSpec((seg, HEAD_DIM), lambda b, h, t: (b, va0 // HEAD_DIM + h)),
            pl.BlockSpec((tq, qw), lambda b, h, t: (b * tiles + t, za0 // qw + h)),
        ],
        out_specs=pl.BlockSpec((tq, qw), lambda b, h, t: (b * tiles + t, h)),
        out_shape=jax.ShapeDtypeStruct((rows, a_width), BF16),
        scratch_shapes=[pltpu.VMEM((1, A_GROUP * tq), F32), pltpu.VMEM((1, A_GROUP * tq), F32),
                        pltpu.VMEM((HEAD_DIM, A_GROUP * tq), F32)],
        compiler_params=_cparams(("arbitrary", "arbitrary", "arbitrary")),
        name="attention_a",
    )(u, u, u, u)


def _attn_c_kernel(n_chunks, one_minus_lambda_init, lambda_init,
                   q_ref, k_ref, v_ref, z_ref, lq1_ref, lk1_ref, lq2_ref, lk2_ref, sg_ref,
                   o_ref, m_ref, l_ref, acc_ref):
    tq = q_ref.shape[0]
    q = q_ref[...]
    q0, q1 = q[:, :HEAD_DIM], q[:, HEAD_DIM:]

    def scores(off, size):
        k = k_ref[pl.ds(off, size), :]
        s0 = lax.dot_general(k[:, :HEAD_DIM], q0, _NT, preferred_element_type=F32)
        s1 = lax.dot_general(k[:, HEAD_DIM:], q1, _NT, preferred_element_type=F32)
        return jnp.concatenate([s0, s1], axis=1)

    _online_softmax(scores, v_ref, m_ref, l_ref, acc_ref, pl.program_id(2) > 0, n_chunks)
    lam = (jnp.exp(jnp.sum(lq1_ref[...] * lk1_ref[...], axis=-1, keepdims=True)) -
           jnp.exp(jnp.sum(lq2_ref[...] * lk2_ref[...], axis=-1, keepdims=True)) + lambda_init)
    ot = acc_ref[...] * (1.0 / l_ref[...])
    od = jnp.transpose(ot[:, :tq] - lam * ot[:, tq:])
    y = od * lax.rsqrt(jnp.mean(od * od, axis=-1, keepdims=True) + EPS) * sg_ref[...]
    y = y * one_minus_lambda_init * _silu(z_ref[...].astype(F32))
    o_ref[...] = y.astype(BF16)


def _attention_c(u, lam_vecs, subln_g, lambda_init, batch, seg, cols):
    rows = u.shape[0]
    kc0, vc0, qc0, zc0, c_width = cols
    heads = c_width // C_VDIM
    tq = ROW_TILE
    tiles = seg // tq
    vec = pl.BlockSpec((1, HEAD_DIM), lambda b, h, t: (0, 0))
    return pl.pallas_call(
        functools.partial(_attn_c_kernel, (seg - KV_CHUNK) // LAT_CHUNK, 1.0 - lambda_init, lambda_init),
        grid=(batch, heads, tiles),
        in_specs=[
            pl.BlockSpec((tq, C_VDIM), lambda b, h, t: (b * tiles + t, qc0 // C_VDIM + h)),
            pl.BlockSpec((seg, C_VDIM), lambda b, h, t: (b, kc0 // C_VDIM + h)),
            pl.BlockSpec((seg, C_VDIM), lambda b, h, t: (b, vc0 // C_VDIM + h)),
            pl.BlockSpec((tq, C_VDIM), lambda b, h, t: (b * tiles + t, zc0 // C_VDIM + h)),
            vec, vec, vec, vec,
            pl.BlockSpec((1, C_VDIM), lambda b, h, t: (0, 0)),
        ],
        out_specs=pl.BlockSpec((tq, C_VDIM), lambda b, h, t: (b * tiles + t, h)),
        out_shape=jax.ShapeDtypeStruct((rows, c_width), BF16),
        scratch_shapes=[pltpu.VMEM((1, 2 * tq), F32), pltpu.VMEM((1, 2 * tq), F32),
                        pltpu.VMEM((C_VDIM, 2 * tq), F32)],
        compiler_params=_cparams(("arbitrary", "arbitrary", "arbitrary")),
        name="attention_c",
    )(u, u, u, u, *lam_vecs, subln_g)


HALO = 16


def _conv_kernel(tiles_per_seg, xb_ref, bb_ref, cb_ref, zb_ref, xp_ref, cp_ref, xn_ref, cn_ref,
                 w_ref, o_ref, g_ref):
    tr = xb_ref.shape[0]
    r = pl.program_id(0) % tiles_per_seg
    has_prev = ((r != 0) & (r != 1)).astype(F32)
    has_next = ((r != 0) & (r != tiles_per_seg - 1)).astype(F32)
    g = cb_ref[...].astype(F32) * xb_ref[...].astype(F32)
    g_ref[0:HALO, :] = cp_ref[...].astype(F32) * xp_ref[...].astype(F32) * has_prev
    g_ref[HALO:HALO + tr, :] = g
    g_ref[HALO + tr:HALO + tr + HALO, :] = cn_ref[...].astype(F32) * xn_ref[...].astype(F32) * has_next
    w = w_ref[...]
    conv = (g_ref[HALO - 1:HALO - 1 + tr, :] * w[0:1] + g * w[1:2] +
            g_ref[HALO + 1:HALO + 1 + tr, :] * w[2:3])
    y = bb_ref[...].astype(F32) * conv * _silu(zb_ref[...].astype(F32))
    o_ref[...] = y.astype(BF16)


def _short_conv(u, conv_w, tiles_per_seg, cols):
    rows = u.shape[0]
    xb0, bb0, cb0, zb0, bw = cols
    tr = ROW_TILE
    n_tiles = rows // tr
    hpt = tr // HALO
    last_halo = rows // HALO - 1
    main = lambda c0: pl.BlockSpec((tr, bw), lambda t: (t, c0 // bw))
    prev = lambda c0: pl.BlockSpec((HALO, bw), lambda t: (jnp.maximum(t * hpt - 1, 0), c0 // bw))
    nxt = lambda c0: pl.BlockSpec((HALO, bw), lambda t: (jnp.minimum((t + 1) * hpt, last_halo), c0 // bw))
    return pl.pallas_call(
        functools.partial(_conv_kernel, tiles_per_seg),
        grid=(n_tiles,),
        in_specs=[main(xb0), main(bb0), main(cb0), main(zb0),
                  prev(xb0), prev(cb0), nxt(xb0), nxt(cb0),
                  pl.BlockSpec(conv_w.shape, lambda t: (0, 0))],
        out_specs=pl.BlockSpec((tr, bw), lambda t: (t, 0)),
        out_shape=jax.ShapeDtypeStruct((rows, bw), BF16),
        scratch_shapes=[pltpu.VMEM((tr + 2 * HALO, bw), F32)],
        compiler_params=_cparams(("arbitrary",)),
        name="short_conv",
    )(u, u, u, u, u, u, u, u, conv_w)


def _out_kernel(seg, ctx_len, oa_ref, yb_ref, oc_ref, w_ref, gate_ref, h_ref, o_ref, wbf_ref):
    @pl.when(pl.program_id(1) == 0)
    def _():
        wbf_ref[...] = w_ref[...].astype(BF16)

    ka = oa_ref.shape[1]
    kb = yb_ref.shape[1]
    acc = jnp.dot(oa_ref[...], wbf_ref[0:ka, :], preferred_element_type=F32)
    acc += jnp.dot(yb_ref[...], wbf_ref[ka:ka + kb, :], preferred_element_type=F32)
    acc += jnp.dot(oc_ref[...], wbf_ref[ka + kb:, :], preferred_element_type=F32)
    tm = acc.shape[0]
    row = pl.program_id(1) * tm + lax.broadcasted_iota(jnp.int32, (tm, 1), 0)
    is_ctx = (row % seg) < ctx_len
    gate = jnp.where(is_ctx, gate_ref[2:3, :], jnp.where(row < seg, gate_ref[0:1, :], gate_ref[1:2, :]))
    o_ref[...] = h_ref[...] + gate * acc


def _out_projection(oa, yb, oc, w_out, layer, gates, h, seg, ctx_len, tm):
    rows, d = h.shape
    tn = PROJ_TN
    kdim = w_out.shape[1]
    return pl.pallas_call(
        functools.partial(_out_kernel, seg, ctx_len),
        grid=(d // tn, rows // tm),
        in_specs=[
            pl.BlockSpec((tm, oa.shape[1]), lambda j, i: (i, 0)),
            pl.BlockSpec((tm, yb.shape[1]), lambda j, i: (i, 0)),
            pl.BlockSpec((tm, oc.shape[1]), lambda j, i: (i, 0)),
            pl.BlockSpec((None, kdim, tn), lambda j, i: (layer, 0, j)),
            pl.BlockSpec((8, tn), lambda j, i: (0, j)),
            pl.BlockSpec((tm, tn), lambda j, i: (i, j)),
        ],
        out_specs=pl.BlockSpec((tm, tn), lambda j, i: (i, j)),
        out_shape=jax.ShapeDtypeStruct((rows, d), F32),
        scratch_shapes=[pltpu.VMEM((kdim, tn), BF16)],
        compiler_params=_cparams(("arbitrary", "arbitrary")),
        name="out_projection",
    )(oa, yb, oc, w_out, gates, h)


def _final_kernel(h_ref, g_ref, o_ref):
    h = h_ref[...]
    ms = jnp.mean(h * h, axis=-1, keepdims=True)
    o_ref[...] = (h * lax.rsqrt(ms + EPS)) * g_ref[...]


def _final_norm(h, g, batch, tiles_per_seg):
    d = h.shape[1]
    lat_tiles = tiles_per_seg - 1
    return pl.pallas_call(
        _final_kernel,
        grid=(batch * lat_tiles,),
        in_specs=[
            pl.BlockSpec((ROW_TILE, d), lambda t: ((t // lat_tiles) * tiles_per_seg + 1 + t % lat_tiles, 0)),
            pl.BlockSpec((1, d), lambda t: (0, 0)),
        ],
        out_specs=pl.BlockSpec((ROW_TILE, d), lambda t: (t, 0)),
        out_shape=jax.ShapeDtypeStruct((batch * lat_tiles * ROW_TILE, d), F32),
        compiler_params=_cparams(("arbitrary",)),
        name="final_norm",
    )(h, g)


def _rope_tables(batch, seq, ctx_len):
    pos = jnp.arange(seq)
    inv = ROPE_THETA ** (-jnp.arange(0, AXIS_DIM, 2, dtype=F32) / AXIS_DIM)
    ar = (pos // GRID_W).astype(F32)[:, None] * inv
    ac = (pos % GRID_W).astype(F32)[:, None] * inv
    cos = jnp.concatenate([jnp.cos(ar), jnp.cos(ar), jnp.cos(ac), jnp.cos(ac)], axis=-1)
    sin = jnp.concatenate([-jnp.sin(ar), jnp.sin(ar), -jnp.sin(ac), jnp.sin(ac)], axis=-1)
    cos = jnp.concatenate([jnp.ones((ctx_len, HEAD_DIM), F32), cos], axis=0)
    sin = jnp.concatenate([jnp.zeros((ctx_len, HEAD_DIM), F32), sin], axis=0)
    return jnp.tile(cos, (batch, 1)), jnp.tile(sin, (batch, 1))


def _largest_tile(rows, cap):
    best = 16
    for t in range(16, cap + 1, 16):
        if rows % t == 0:
            best = t
    return best


def kernel(x, c, ctx, c_ctx, w_mod, b_mod, norm_g, w_in, q_norm_a, k_norm_a, conv_w, lambda_q1,
           lambda_k1, lambda_q2, lambda_k2, subln_g, w_out, final_g):
    batch, seq, d = x.shape
    ctx_len = ctx.shape[1]
    depth = w_in.shape[0]
    seg = ctx_len + seq
    tiles_per_seg = seg // ROW_TILE
    assert ctx_len == ROW_TILE == KV_CHUNK and seq % ROW_TILE == 0

    a_width, b_width, c_width = d // 2, d // 4, d // 4
    kv_a = a_width // A_GROUP
    ka0 = 0
    va0 = ka0 + kv_a
    kc0 = va0 + kv_a
    vc0 = kc0 + c_width
    qa0 = vc0 + c_width
    qc0 = qa0 + a_width
    xb0 = qc0 + c_width
    bb0 = xb0 + b_width
    cb0 = bb0 + b_width
    za0 = cb0 + b_width
    zb0 = za0 + a_width
    zc0 = zb0 + b_width
    assert zc0 + c_width == w_in.shape[2]
    col_groups = tuple(v // PROJ_TN for v in (va0, kc0, vc0, qa0, qc0, xb0))

    cvecs = jnp.concatenate([c, c_ctx[None, :], jnp.zeros((8 - batch - 1, d), F32)], axis=0)
    mods = _modulation(cvecs, w_mod, b_mod)
    cos, sin = _rope_tables(batch, seq, ctx_len)
    rows = batch * seg
    tm = _largest_tile(rows, 1088)

    h = None
    for i in range(depth):
        lambda_init = 0.8 - 0.6 * math.exp(-0.3 * i)
        shift = mods[i, :, 0:d].reshape(8, 1, d)
        scale = mods[i, :, d:2 * d].reshape(8, 1, d)
        gates = mods[i, :, 2 * d:3 * d]
        g = norm_g[i].reshape(1, d)
        if i == 0:
            n, h = _norm_layer0(x.reshape(batch * seq, d), ctx.reshape(batch * ctx_len, d), g,
                                scale, shift, tiles_per_seg)
        else:
            n = _norm_layer1(h, g, scale, shift, tiles_per_seg)
        u = _in_projection(n, w_in, i, cos, sin, q_norm_a[i].reshape(1, HEAD_DIM),
                           k_norm_a[i].reshape(1, HEAD_DIM), col_groups, tm)
        oa = _attention_a(u, batch, seg, (ka0, va0, qa0, za0, a_width))
        lam_vecs = [v[i].reshape(1, HEAD_DIM) for v in (lambda_q1, lambda_k1, lambda_q2, lambda_k2)]
        oc = _attention_c(u, lam_vecs, subln_g[i].reshape(1, C_VDIM), lambda_init, batch, seg,
                          (kc0, vc0, qc0, zc0, c_width))
        yb = _short_conv(u, conv_w[i], tiles_per_seg, (xb0, bb0, cb0, zb0, b_width))
        h = _out_projection(oa, yb, oc, w_out, i, gates, h, seg, ctx_len, tm)
    out = _final_norm(h, final_g.reshape(1, d), batch, tiles_per_seg)
    return out.reshape(batch, seq, d)
```

```python
import functools
import math

import jax
import jax.numpy as jnp
from jax import lax
from jax.experimental import pallas as pl
from jax.experimental.pallas import tpu as pltpu

GRID_W = 64
HEAD_DIM = 128
AXIS_DIM = HEAD_DIM // 2
ROPE_THETA = 10000.0
EPS = 1e-6
ATTN_SCALE = 1.0 / math.sqrt(HEAD_DIM)
Q_PRESCALE = ATTN_SCALE * math.log2(math.e)
A_GROUP = 4
C_VDIM = 2 * HEAD_DIM

F32 = jnp.float32
BF16 = jnp.bfloat16

VMEM_LIMIT_BYTES = 56 * 1024 * 1024
ROW_TILE = 256
KV_CHUNK = 256
A_LAT_CHUNK = 1024
C_LAT_CHUNK = 256
PROJ_TN = 512
PROJ_SUBTILES = 4


def _cparams(sem):
    return pltpu.CompilerParams(dimension_semantics=sem, vmem_limit_bytes=VMEM_LIMIT_BYTES)


def _silu(x):
    return x * (1.0 / (1.0 + jnp.exp(-x)))


def _mod_kernel(c_ref, w_ref, b_ref, o_ref):
    s = _silu(c_ref[...]).astype(BF16)
    w = w_ref[0].astype(BF16)
    o_ref[0] = jnp.dot(s, w, preferred_element_type=F32) + b_ref[0]


def _modulation(cvecs, w_mod, b_mod):
    depth, d, n3 = w_mod.shape
    tn = PROJ_TN
    return pl.pallas_call(
        _mod_kernel,
        grid=(depth, n3 // tn),
        in_specs=[
            pl.BlockSpec((8, d), lambda l, j: (0, 0)),
            pl.BlockSpec((1, d, tn), lambda l, j: (l, 0, j)),
            pl.BlockSpec((1, 1, tn), lambda l, j: (l, 0, j)),
        ],
        out_specs=pl.BlockSpec((1, 8, tn), lambda l, j: (l, 0, j)),
        out_shape=jax.ShapeDtypeStruct((depth, 8, n3), F32),
        compiler_params=_cparams(("arbitrary", "arbitrary")),
        name="modulation",
    )(cvecs, w_mod, b_mod.reshape(depth, 1, n3))


def _norm_math(h, g, scale, shift):
    ms = jnp.mean(h * h, axis=-1, keepdims=True)
    return ((h * lax.rsqrt(ms + EPS)) * g * (1.0 + scale) + shift).astype(BF16)


def _norm0_kernel(tiles_per_seg, x_ref, c_ref, g_ref, sc_ref, sh_ref, n_ref, h_ref):
    is_ctx = (pl.program_id(0) % tiles_per_seg) == 0
    h = jnp.where(is_ctx, c_ref[...], x_ref[...])
    h_ref[...] = h
    n_ref[...] = _norm_math(h, g_ref[...], sc_ref[0], sh_ref[0])


def _norm1_kernel(h_ref, g_ref, sc_ref, sh_ref, n_ref):
    n_ref[...] = _norm_math(h_ref[...], g_ref[...], sc_ref[0], sh_ref[0])


def _mod_row(t, tiles_per_seg):
    return jnp.where(t % tiles_per_seg == 0, 2, t // tiles_per_seg)


def _norm_layer0(x2, c2, g, scale, shift, tiles_per_seg):
    d = x2.shape[1]
    lat_tiles = tiles_per_seg - 1
    n_tiles = (x2.shape[0] + c2.shape[0]) // ROW_TILE
    rows = n_tiles * ROW_TILE
    xmap = lambda t: ((t // tiles_per_seg) * lat_tiles + jnp.maximum(t % tiles_per_seg - 1, 0), 0)
    mrow = lambda t: (_mod_row(t, tiles_per_seg), 0, 0)
    return pl.pallas_call(
        functools.partial(_norm0_kernel, tiles_per_seg),
        grid=(n_tiles,),
        in_specs=[
            pl.BlockSpec((ROW_TILE, d), xmap),
            pl.BlockSpec((ROW_TILE, d), lambda t: (t // tiles_per_seg, 0)),
            pl.BlockSpec((1, d), lambda t: (0, 0)),
            pl.BlockSpec((1, 1, d), mrow),
            pl.BlockSpec((1, 1, d), mrow),
        ],
        out_specs=[pl.BlockSpec((ROW_TILE, d), lambda t: (t, 0)),
                   pl.BlockSpec((ROW_TILE, d), lambda t: (t, 0))],
        out_shape=[jax.ShapeDtypeStruct((rows, d), BF16), jax.ShapeDtypeStruct((rows, d), F32)],
        compiler_params=_cparams(("arbitrary",)),
        name="norm_layer0",
    )(x2, c2, g, scale, shift)


def _norm_layer1(h, g, scale, shift, tiles_per_seg):
    rows, d = h.shape
    mrow = lambda t: (_mod_row(t, tiles_per_seg), 0, 0)
    return pl.pallas_call(
        _norm1_kernel,
        grid=(rows // ROW_TILE,),
        in_specs=[
            pl.BlockSpec((ROW_TILE, d), lambda t: (t, 0)),
            pl.BlockSpec((1, d), lambda t: (0, 0)),
            pl.BlockSpec((1, 1, d), mrow),
            pl.BlockSpec((1, 1, d), mrow),
        ],
        out_specs=pl.BlockSpec((ROW_TILE, d), lambda t: (t, 0)),
        out_shape=jax.ShapeDtypeStruct((rows, d), BF16),
        compiler_params=_cparams(("arbitrary",)),
        name="norm_layer1",
    )(h, g, scale, shift)


def _rope_partner(x):
    lane = lax.broadcasted_iota(jnp.int32, x.shape, 1)
    fwd = pltpu.roll(x, HEAD_DIM - AXIS_DIM // 2, axis=1)
    bwd = pltpu.roll(x, AXIS_DIM // 2, axis=1)
    return jnp.where((lane % AXIS_DIM) < AXIS_DIM // 2, fwd, bwd)


def _proj_kernel(col_groups, n_ref, w_ref, cos_ref, sin_ref, qg_ref, kg_ref, o_ref, wbf_ref):
    ka_end, va_end, kc_end, vc_end, qa_end, qc_end = col_groups
    j = pl.program_id(0)

    @pl.when(pl.program_id(1) == 0)
    def _():
        wbf_ref[...] = w_ref[...].astype(BF16)

    tm, tn = o_ref.shape
    heads = tn // HEAD_DIM
    is_ka = j < ka_end
    is_qa = (j >= vc_end) & (j < qa_end)
    is_kc = (j >= va_end) & (j < kc_end)
    is_qc = (j >= qa_end) & (j < qc_end)
    is_q = j >= vc_end
    qscale = jnp.where(is_q, Q_PRESCALE, 1.0).astype(F32)

    def tiles(epilogue):
        sub = tm // PROJ_SUBTILES
        spans = [pl.ds(r * sub, sub) for r in range(PROJ_SUBTILES)]
        dot = lambda rows: jnp.dot(n_ref[rows, :], wbf_ref[...], preferred_element_type=F32)
        acc_next = dot(spans[0])
        for r, rows in enumerate(spans):
            acc = acc_next
            if r + 1 < len(spans):
                acc_next = dot(spans[r + 1])
            epilogue(acc, rows)

    def rotary(x, cs, ss):
        return x * cs + _rope_partner(x) * ss

    def norm_rope(acc, rows):
        g = jnp.where(is_ka, kg_ref[...], qg_ref[...])
        cs = cos_ref[rows, :] * qscale
        ss = sin_ref[rows, :] * qscale
        xs = [acc[:, hd * HEAD_DIM:(hd + 1) * HEAD_DIM] for hd in range(heads)]
        rs = [lax.rsqrt(jnp.mean(x * x, axis=-1, keepdims=True) + EPS) for x in xs]
        for hd in range(heads):
            y = rotary(xs[hd] * g, cs, ss) * rs[hd]
            o_ref[rows, hd * HEAD_DIM:(hd + 1) * HEAD_DIM] = y.astype(BF16)

    def rope(acc, rows):
        cs = cos_ref[rows, :] * qscale
        ss = sin_ref[rows, :] * qscale
        for hd in range(heads):
            x = acc[:, hd * HEAD_DIM:(hd + 1) * HEAD_DIM]
            o_ref[rows, hd * HEAD_DIM:(hd + 1) * HEAD_DIM] = rotary(x, cs, ss).astype(BF16)

    def plain(acc, rows):
        o_ref[rows, :] = acc.astype(BF16)

    pl.when(is_ka | is_qa)(lambda: tiles(norm_rope))
    pl.when(is_kc | is_qc)(lambda: tiles(rope))
    pl.when(jnp.logical_not(is_ka | is_qa | is_kc | is_qc))(lambda: tiles(plain))


def _in_projection(n, w_in, layer, cos, sin, qg, kg, col_groups, tm):
    rows, d = n.shape
    cols = w_in.shape[2]
    tn = PROJ_TN
    return pl.pallas_call(
        functools.partial(_proj_kernel, col_groups),
        grid=(cols // tn, rows // tm),
        in_specs=[
            pl.BlockSpec((tm, d), lambda j, i: (i, 0)),
            pl.BlockSpec((None, d, tn), lambda j, i: (layer, 0, j)),
            pl.BlockSpec((tm, HEAD_DIM), lambda j, i: (i, 0)),
            pl.BlockSpec((tm, HEAD_DIM), lambda j, i: (i, 0)),
            pl.BlockSpec((1, HEAD_DIM), lambda j, i: (0, 0)),
            pl.BlockSpec((1, HEAD_DIM), lambda j, i: (0, 0)),
        ],
        out_specs=pl.BlockSpec((tm, tn), lambda j, i: (i, j)),
        out_shape=jax.ShapeDtypeStruct((rows, cols), BF16),
        scratch_shapes=[pltpu.VMEM((d, tn), BF16)],
        compiler_params=_cparams(("arbitrary", "arbitrary")),
        name="in_projection",
    )(n, w_in, cos, sin, qg, kg)


_NT = (((1,), (1,)), ((), ()))
_TN = (((0,), (0,)), ((), ()))


def _online_softmax(score_fns, v_ref, m_ref, l_ref, acc_ref, is_latent, lat_chunk):
    def update(g, st, v, first):
        qg = st.shape[1]
        sl = slice(g * qg, (g + 1) * qg)
        m_new = jnp.max(st, axis=0, keepdims=True)
        if not first:
            m_old = m_ref[:, sl]
            m_new = jnp.maximum(m_old, m_new)
            alpha = jnp.exp2(m_old - m_new)
        p = jnp.exp2(st - m_new)
        l_new = jnp.sum(p, axis=0, keepdims=True)
        acc_new = lax.dot_general(v, p.astype(BF16), _TN, preferred_element_type=F32)
        if not first:
            l_new += alpha * l_ref[:, sl]
            acc_new += alpha * acc_ref[:, sl]
        m_ref[:, sl] = m_new
        l_ref[:, sl] = l_new
        acc_ref[:, sl] = acc_new

    def sweep(chunks):
        steps = [(c, g) for c in range(len(chunks)) for g in range(len(score_fns))]
        st_next = score_fns[0](*chunks[0])
        for i, (c, g) in enumerate(steps):
            st = st_next
            if i + 1 < len(steps):
                cn, gn = steps[i + 1]
                st_next = score_fns[gn](*chunks[cn])
            off, size = chunks[c]
            update(g, st, v_ref[pl.ds(off, size), :], c == 0)

    ctx_chunk = [(0, KV_CHUNK)]

    @pl.when(jnp.logical_not(is_latent))
    def _():
        sweep(ctx_chunk)

    @pl.when(is_latent)
    def _():
        n_keys = v_ref.shape[0]
        sweep(ctx_chunk + [(off, lat_chunk) for off in range(KV_CHUNK, n_keys, lat_chunk)])


def _attn_a_kernel(lat_chunk, q_ref, k_ref, v_ref, z_ref, o_ref, m_ref, l_ref, acc_ref):
    tq = q_ref.shape[0]
    q = q_ref[...]
    q4 = jnp.concatenate([q[:, g * HEAD_DIM:(g + 1) * HEAD_DIM] for g in range(A_GROUP)], axis=0)

    def scores(off, size):
        return lax.dot_general(k_ref[pl.ds(off, size), :], q4, _NT,
                               preferred_element_type=F32)

    _online_softmax([scores], v_ref, m_ref, l_ref, acc_ref, pl.program_id(2) > 0, lat_chunk)
    o = jnp.transpose(acc_ref[...] * (1.0 / l_ref[...]))
    z = z_ref[...].astype(F32)
    for g in range(A_GROUP):
        og = o[g * tq:(g + 1) * tq] * _silu(z[:, g * HEAD_DIM:(g + 1) * HEAD_DIM])
        o_ref[:, g * HEAD_DIM:(g + 1) * HEAD_DIM] = og.astype(BF16)


def _attention_a(u, batch, seg, cols):
    rows = u.shape[0]
    ka0, va0, qa0, za0, a_width = cols
    kv_heads = a_width // (A_GROUP * HEAD_DIM)
    tq = ROW_TILE
    qw = A_GROUP * HEAD_DIM
    tiles = seg // tq
    return pl.pallas_call(
        functools.partial(_attn_a_kernel, A_LAT_CHUNK),
        grid=(batch, kv_heads, tiles),
        in_specs=[
            pl.BlockSpec((tq, qw), lambda b, h, t: (b * tiles + t, qa0 // qw + h)),
            pl.BlockSpec((seg, HEAD_DIM), lambda b, h, t: (b, ka0 // HEAD_DIM + h)),
            pl.BlockSpec((seg, HEAD_DIM), lambda b, h, t: (b, va0 // HEAD_DIM + h)),
            pl.BlockSpec((tq, qw), lambda b, h, t: (b * tiles + t, za0 // qw + h)),
        ],
        out_specs=pl.BlockSpec((tq, qw), lambda b, h, t: (b * tiles + t, h)),
        out_shape=jax.ShapeDtypeStruct((rows, a_width), BF16),
        scratch_shapes=[pltpu.VMEM((1, A_GROUP * tq), F32), pltpu.VMEM((1, A_GROUP * tq), F32),
                        pltpu.VMEM((HEAD_DIM, A_GROUP * tq), F32)],
        compiler_params=_cparams(("arbitrary", "arbitrary", "arbitrary")),
        name="attention_a",
    )(u, u, u, u)


def _attn_c_kernel(lat_chunk, one_minus_lambda_init, lambda_init,
                   q_ref, k_ref, v_ref, z_ref, lq1_ref, lk1_ref, lq2_ref, lk2_ref, sg_ref,
                   o_ref, m_ref, l_ref, acc_ref):
    tq = q_ref.shape[0]
    q = q_ref[...]
    q0, q1 = q[:, :HEAD_DIM], q[:, HEAD_DIM:]

    def map_scores(qc, lo):
        return lambda off, size: lax.dot_general(k_ref[pl.ds(off, size), lo:lo + HEAD_DIM], qc, _NT,
                                                 preferred_element_type=F32)

    _online_softmax([map_scores(q0, 0), map_scores(q1, HEAD_DIM)], v_ref, m_ref, l_ref, acc_ref,
                    pl.program_id(2) > 0, lat_chunk)
    lam = (jnp.exp(jnp.sum(lq1_ref[...] * lk1_ref[...], axis=-1, keepdims=True)) -
           jnp.exp(jnp.sum(lq2_ref[...] * lk2_ref[...], axis=-1, keepdims=True)) + lambda_init)
    ot = acc_ref[...] * (1.0 / l_ref[...])
    od = jnp.transpose(ot[:, :tq] - lam * ot[:, tq:])
    y = od * lax.rsqrt(jnp.mean(od * od, axis=-1, keepdims=True) + EPS) * sg_ref[...]
    y = y * one_minus_lambda_init * _silu(z_ref[...].astype(F32))
    o_ref[...] = y.astype(BF16)


def _attention_c(u, lam_vecs, subln_g, lambda_init, batch, seg, cols):
    rows = u.shape[0]
    kc0, vc0, qc0, zc0, c_width = cols
    heads = c_width // C_VDIM
    tq = ROW_TILE
    tiles = seg // tq
    vec = pl.BlockSpec((1, HEAD_DIM), lambda b, h, t: (0, 0))
    return pl.pallas_call(
        functools.partial(_attn_c_kernel, C_LAT_CHUNK, 1.0 - lambda_init, lambda_init),
        grid=(batch, heads, tiles),
        in_specs=[
            pl.BlockSpec((tq, C_VDIM), lambda b, h, t: (b * tiles + t, qc0 // C_VDIM + h)),
            pl.BlockSpec((seg, C_VDIM), lambda b, h, t: (b, kc0 // C_VDIM + h)),
            pl.BlockSpec((seg, C_VDIM), lambda b, h, t: (b, vc0 // C_VDIM + h)),
            pl.BlockSpec((tq, C_VDIM), lambda b, h, t: (b * tiles + t, zc0 // C_VDIM + h)),
            vec, vec, vec, vec,
            pl.BlockSpec((1, C_VDIM), lambda b, h, t: (0, 0)),
        ],
        out_specs=pl.BlockSpec((tq, C_VDIM), lambda b, h, t: (b * tiles + t, h)),
        out_shape=jax.ShapeDtypeStruct((rows, c_width), BF16),
        scratch_shapes=[pltpu.VMEM((1, 2 * tq), F32), pltpu.VMEM((1, 2 * tq), F32),
                        pltpu.VMEM((C_VDIM, 2 * tq), F32)],
        compiler_params=_cparams(("arbitrary", "arbitrary", "arbitrary")),
        name="attention_c",
    )(u, u, u, u, *lam_vecs, subln_g)


HALO = 16


def _conv_kernel(tiles_per_seg, xb_ref, bb_ref, cb_ref, zb_ref, xp_ref, cp_ref, xn_ref, cn_ref,
                 w_ref, o_ref, g_ref):
    tr = xb_ref.shape[0]
    r = pl.program_id(0) % tiles_per_seg
    has_prev = ((r != 0) & (r != 1)).astype(F32)
    has_next = ((r != 0) & (r != tiles_per_seg - 1)).astype(F32)
    g = cb_ref[...].astype(F32) * xb_ref[...].astype(F32)
    g_ref[0:HALO, :] = cp_ref[...].astype(F32) * xp_ref[...].astype(F32) * has_prev
    g_ref[HALO:HALO + tr, :] = g
    g_ref[HALO + tr:HALO + tr + HALO, :] = cn_ref[...].astype(F32) * xn_ref[...].astype(F32) * has_next
    w = w_ref[...]
    conv = (g_ref[HALO - 1:HALO - 1 + tr, :] * w[0:1] + g * w[1:2] +
            g_ref[HALO + 1:HALO + 1 + tr, :] * w[2:3])
    y = bb_ref[...].astype(F32) * conv * _silu(zb_ref[...].astype(F32))
    o_ref[...] = y.astype(BF16)


def _short_conv(u, conv_w, tiles_per_seg, cols):
    rows = u.shape[0]
    xb0, bb0, cb0, zb0, bw = cols
    tr = ROW_TILE
    n_tiles = rows // tr
    hpt = tr // HALO
    last_halo = rows // HALO - 1
    main = lambda c0: pl.BlockSpec((tr, bw), lambda t: (t, c0 // bw))
    prev = lambda c0: pl.BlockSpec((HALO, bw), lambda t: (jnp.maximum(t * hpt - 1, 0), c0 // bw))
    nxt = lambda c0: pl.BlockSpec((HALO, bw), lambda t: (jnp.minimum((t + 1) * hpt, last_halo), c0 // bw))
    return pl.pallas_call(
        functools.partial(_conv_kernel, tiles_per_seg),
        grid=(n_tiles,),
        in_specs=[main(xb0), main(bb0), main(cb0), main(zb0),
                  prev(xb0), prev(cb0), nxt(xb0), nxt(cb0),
                  pl.BlockSpec(conv_w.shape, lambda t: (0, 0))],
        out_specs=pl.BlockSpec((tr, bw), lambda t: (t, 0)),
        out_shape=jax.ShapeDtypeStruct((rows, bw), BF16),
        scratch_shapes=[pltpu.VMEM((tr + 2 * HALO, bw), F32)],
        compiler_params=_cparams(("arbitrary",)),
        name="short_conv",
    )(u, u, u, u, u, u, u, u, conv_w)


def _out_kernel(seg, ctx_len, oa_ref, yb_ref, oc_ref, w_ref, gate_ref, h_ref, o_ref, wbf_ref):
    @pl.when(pl.program_id(1) == 0)
    def _():
        wbf_ref[...] = w_ref[...].astype(BF16)

    ka = oa_ref.shape[1]
    kb = yb_ref.shape[1]
    tm = o_ref.shape[0]
    sub = tm // PROJ_SUBTILES

    def dot(rows):
        acc = jnp.dot(oa_ref[rows, :], wbf_ref[0:ka, :], preferred_element_type=F32)
        acc += jnp.dot(yb_ref[rows, :], wbf_ref[ka:ka + kb, :], preferred_element_type=F32)
        return acc + jnp.dot(oc_ref[rows, :], wbf_ref[ka + kb:, :], preferred_element_type=F32)

    spans = [pl.ds(r * sub, sub) for r in range(PROJ_SUBTILES)]
    acc_next = dot(spans[0])
    for r, rows in enumerate(spans):
        acc = acc_next
        if r + 1 < len(spans):
            acc_next = dot(spans[r + 1])
        row = pl.program_id(1) * tm + r * sub + lax.broadcasted_iota(jnp.int32, (sub, 1), 0)
        is_ctx = (row % seg) < ctx_len
        gate = jnp.where(is_ctx, gate_ref[2:3, :], jnp.where(row < seg, gate_ref[0:1, :], gate_ref[1:2, :]))
        o_ref[rows, :] = h_ref[rows, :] + gate * acc


def _out_projection(oa, yb, oc, w_out, layer, gates, h, seg, ctx_len, tm):
    rows, d = h.shape
    tn = PROJ_TN
    kdim = w_out.shape[1]
    return pl.pallas_call(
        functools.partial(_out_kernel, seg, ctx_len),
        grid=(d // tn, rows // tm),
        in_specs=[
            pl.BlockSpec((tm, oa.shape[1]), lambda j, i: (i, 0)),
            pl.BlockSpec((tm, yb.shape[1]), lambda j, i: (i, 0)),
            pl.BlockSpec((tm, oc.shape[1]), lambda j, i: (i, 0)),
            pl.BlockSpec((None, kdim, tn), lambda j, i: (layer, 0, j)),
            pl.BlockSpec((8, tn), lambda j, i: (0, j)),
            pl.BlockSpec((tm, tn), lambda j, i: (i, j)),
        ],
        out_specs=pl.BlockSpec((tm, tn), lambda j, i: (i, j)),
        out_shape=jax.ShapeDtypeStruct((rows, d), F32),
        scratch_shapes=[pltpu.VMEM((kdim, tn), BF16)],
        compiler_params=_cparams(("arbitrary", "arbitrary")),
        name="out_projection",
    )(oa, yb, oc, w_out, gates, h)


def _final_kernel(h_ref, g_ref, o_ref):
    h = h_ref[...]
    ms = jnp.mean(h * h, axis=-1, keepdims=True)
    o_ref[...] = (h * lax.rsqrt(ms + EPS)) * g_ref[...]


def _final_norm(h, g, batch, tiles_per_seg):
    d = h.shape[1]
    lat_tiles = tiles_per_seg - 1
    return pl.pallas_call(
        _final_kernel,
        grid=(batch * lat_tiles,),
        in_specs=[
            pl.BlockSpec((ROW_TILE, d), lambda t: ((t // lat_tiles) * tiles_per_seg + 1 + t % lat_tiles, 0)),
            pl.BlockSpec((1, d), lambda t: (0, 0)),
        ],
        out_specs=pl.BlockSpec((ROW_TILE, d), lambda t: (t, 0)),
        out_shape=jax.ShapeDtypeStruct((batch * lat_tiles * ROW_TILE, d), F32),
        compiler_params=_cparams(("arbitrary",)),
        name="final_norm",
    )(h, g)


def _rope_tables(batch, seq, ctx_len):
    pos = jnp.arange(seq)
    inv = ROPE_THETA ** (-jnp.arange(0, AXIS_DIM, 2, dtype=F32) / AXIS_DIM)
    ar = (pos // GRID_W).astype(F32)[:, None] * inv
    ac = (pos % GRID_W).astype(F32)[:, None] * inv
    cos = jnp.concatenate([jnp.cos(ar), jnp.cos(ar), jnp.cos(ac), jnp.cos(ac)], axis=-1)
    sin = jnp.concatenate([-jnp.sin(ar), jnp.sin(ar), -jnp.sin(ac), jnp.sin(ac)], axis=-1)
    cos = jnp.concatenate([jnp.ones((ctx_len, HEAD_DIM), F32), cos], axis=0)
    sin = jnp.concatenate([jnp.zeros((ctx_len, HEAD_DIM), F32), sin], axis=0)
    return jnp.tile(cos, (batch, 1)), jnp.tile(sin, (batch, 1))


def _largest_tile(rows, cap):
    best = 16
    for t in range(16, cap + 1, 16):
        if rows % t == 0:
            best = t
    return best


def kernel(x, c, ctx, c_ctx, w_mod, b_mod, norm_g, w_in, q_norm_a, k_norm_a, conv_w, lambda_q1,
           lambda_k1, lambda_q2, lambda_k2, subln_g, w_out, final_g):
    batch, seq, d = x.shape
    ctx_len = ctx.shape[1]
    depth = w_in.shape[0]
    seg = ctx_len + seq
    tiles_per_seg = seg // ROW_TILE
    assert ctx_len == ROW_TILE == KV_CHUNK and seq % ROW_TILE == 0

    a_width, b_width, c_width = d // 2, d // 4, d // 4
    kv_a = a_width // A_GROUP
    ka0 = 0
    va0 = ka0 + kv_a
    kc0 = va0 + kv_a
    vc0 = kc0 + c_width
    qa0 = vc0 + c_width
    qc0 = qa0 + a_width
    xb0 = qc0 + c_width
    bb0 = xb0 + b_width
    cb0 = bb0 + b_width
    za0 = cb0 + b_width
    zb0 = za0 + a_width
    zc0 = zb0 + b_width
    assert zc0 + c_width == w_in.shape[2]
    col_groups = tuple(v // PROJ_TN for v in (va0, kc0, vc0, qa0, qc0, xb0))

    cvecs = jnp.concatenate([c, c_ctx[None, :], jnp.zeros((8 - batch - 1, d), F32)], axis=0)
    mods = _modulation(cvecs, w_mod, b_mod)
    cos, sin = _rope_tables(batch, seq, ctx_len)
    rows = batch * seg
    tm = _largest_tile(rows, 1088)

    h = None
    for i in range(depth):
        lambda_init = 0.8 - 0.6 * math.exp(-0.3 * i)
        shift = mods[i, :, 0:d].reshape(8, 1, d)
        scale = mods[i, :, d:2 * d].reshape(8, 1, d)
        gates = mods[i, :, 2 * d:3 * d]
        g = norm_g[i].reshape(1, d)
        if i == 0:
            n, h = _norm_layer0(x.reshape(batch * seq, d), ctx.reshape(batch * ctx_len, d), g,
                                scale, shift, tiles_per_seg)
        else:
            n = _norm_layer1(h, g, scale, shift, tiles_per_seg)
        u = _in_projection(n, w_in, i, cos, sin, q_norm_a[i].reshape(1, HEAD_DIM),
                           k_norm_a[i].reshape(1, HEAD_DIM), col_groups, tm)
        oa = _attention_a(u, batch, seg, (ka0, va0, qa0, za0, a_width))
        lam_vecs = [v[i].reshape(1, HEAD_DIM) for v in (lambda_q1, lambda_k1, lambda_q2, lambda_k2)]
        oc = _attention_c(u, lam_vecs, subln_g[i].reshape(1, C_VDIM), lambda_init, batch, seg,
                          (kc0, vc0, qc0, zc0, c_width))
        yb = _short_conv(u, conv_w[i], tiles_per_seg, (xb0, bb0, cb0, zb0, b_width))
        h = _out_projection(oa, yb, oc, w_out, i, gates, h, seg, ctx_len, tm)
    out = _final_norm(h, final_g.reshape(1, d), batch, tiles_per_seg)
    return out.reshape(batch, seq, d)
```

```python
import functools
import math

import jax
import jax.numpy as jnp
from jax import lax
from jax.experimental import pallas as pl
from jax.experimental.pallas import tpu as pltpu

GRID_W = 64
HEAD_DIM = 128
AXIS_DIM = HEAD_DIM // 2
ROPE_THETA = 10000.0
EPS = 1e-6
ATTN_SCALE = 1.0 / math.sqrt(HEAD_DIM)
Q_PRESCALE = ATTN_SCALE * math.log2(math.e)
A_GROUP = 4
C_VDIM = 2 * HEAD_DIM

F32 = jnp.float32
BF16 = jnp.bfloat16

VMEM_LIMIT_BYTES = 56 * 1024 * 1024
IN_PROJ_VMEM_LIMIT_BYTES = 61 * 1024 * 1024
ROW_TILE = 256
KV_CHUNK = 256
A_LAT_CHUNK = 1024
C_LAT_CHUNK = 256
PROJ_TN = 512
IN_PROJ_TN = 1024
PROJ_TM = 512
PROJ_SUBTILES = 4


def _cparams(sem, vmem_limit_bytes=VMEM_LIMIT_BYTES):
    return pltpu.CompilerParams(dimension_semantics=sem, vmem_limit_bytes=vmem_limit_bytes)


def _silu(x):
    return x * (1.0 / (1.0 + jnp.exp(-x)))


def _mod_kernel(c_ref, w_ref, b_ref, o_ref):
    s = _silu(c_ref[...]).astype(BF16)
    w = w_ref[0].astype(BF16)
    o_ref[0] = jnp.dot(s, w, preferred_element_type=F32) + b_ref[0]


def _modulation(cvecs, w_mod, b_mod):
    depth, d, n3 = w_mod.shape
    tn = PROJ_TN
    return pl.pallas_call(
        _mod_kernel,
        grid=(depth, n3 // tn),
        in_specs=[
            pl.BlockSpec((8, d), lambda l, j: (0, 0)),
            pl.BlockSpec((1, d, tn), lambda l, j: (l, 0, j)),
            pl.BlockSpec((1, 1, tn), lambda l, j: (l, 0, j)),
        ],
        out_specs=pl.BlockSpec((1, 8, tn), lambda l, j: (l, 0, j)),
        out_shape=jax.ShapeDtypeStruct((depth, 8, n3), F32),
        compiler_params=_cparams(("arbitrary", "arbitrary")),
        name="modulation",
    )(cvecs, w_mod, b_mod.reshape(depth, 1, n3))


def _norm_math(h, g, scale, shift):
    ms = jnp.mean(h * h, axis=-1, keepdims=True)
    return ((h * lax.rsqrt(ms + EPS)) * g * (1.0 + scale) + shift).astype(BF16)


def _norm0_kernel(lat_tiles, x_ref, c_ref, g_ref, sc_ref, sh_ref, n_ref):
    h = jnp.where(pl.program_id(0) >= lat_tiles, c_ref[...], x_ref[...])
    n_ref[...] = _norm_math(h, g_ref[...], sc_ref[0], sh_ref[0])


def _norm1_kernel(h_ref, g_ref, sc_ref, sh_ref, n_ref):
    n_ref[...] = _norm_math(h_ref[...], g_ref[...], sc_ref[0], sh_ref[0])


def _mod_row(t, lat_tiles, tiles_per_batch, ctx_row):
    return jnp.where(t >= lat_tiles, ctx_row, t // tiles_per_batch)


def _adaln_norm(x2, c2, h, g, scale, shift, batch):
    first = h is None
    d = x2.shape[1]
    lat_tiles = x2.shape[0] // ROW_TILE
    n_tiles = lat_tiles + c2.shape[0] // ROW_TILE
    mrow = lambda t: (_mod_row(t, lat_tiles, lat_tiles // batch, batch), 0, 0)
    tile = lambda imap: pl.BlockSpec((ROW_TILE, d), imap)
    if first:
        body = functools.partial(_norm0_kernel, lat_tiles)
        acts = [x2, c2]
        act_specs = [tile(lambda t: (jnp.minimum(t, lat_tiles - 1), 0)),
                     tile(lambda t: (jnp.maximum(t - lat_tiles, 0), 0))]
    else:
        body, acts, act_specs = _norm1_kernel, [h], [tile(lambda t: (t, 0))]
    return pl.pallas_call(
        body,
        grid=(n_tiles,),
        in_specs=act_specs + [pl.BlockSpec((1, d), lambda t: (0, 0)),
                              pl.BlockSpec((1, 1, d), mrow), pl.BlockSpec((1, 1, d), mrow)],
        out_specs=tile(lambda t: (t, 0)),
        out_shape=jax.ShapeDtypeStruct((n_tiles * ROW_TILE, d), BF16),
        compiler_params=_cparams(("arbitrary",)),
        name="adaln_norm",
    )(*acts, g, scale, shift)


def _rope_partner(x):
    lane = lax.broadcasted_iota(jnp.int32, x.shape, 1)
    fwd = pltpu.roll(x, HEAD_DIM - AXIS_DIM // 2, axis=1)
    bwd = pltpu.roll(x, AXIS_DIM // 2, axis=1)
    return jnp.where((lane % AXIS_DIM) < AXIS_DIM // 2, fwd, bwd)


def _proj_kernel(kinds, q_tile0, ctx_skip, n_ref, w_ref, cos_ref, sin_ref, qg_ref, kg_ref, o_ref, wbf_ref):
    j = pl.program_id(0)

    @pl.when(pl.program_id(1) == 0)
    def _():
        wbf_ref[...] = w_ref[...].astype(BF16)

    tm, tn = o_ref.shape
    is_q = j >= q_tile0
    qscale = jnp.where(is_q, Q_PRESCALE, 1.0).astype(F32)

    def rotary(x, cs, ss):
        return x * cs + _rope_partner(x) * ss

    def epilogue(pattern, acc, rows):
        if all(kind == "plain" for kind in pattern):
            o_ref[rows, :] = acc.astype(BF16)
            return
        g = jnp.where(is_q, qg_ref[...], kg_ref[...])
        cs = cos_ref[rows, :] * qscale
        ss = sin_ref[rows, :] * qscale
        xs = [acc[:, hd * HEAD_DIM:(hd + 1) * HEAD_DIM] for hd in range(len(pattern))]
        rs = [lax.rsqrt(jnp.mean(x * x, axis=-1, keepdims=True) + EPS) if kind == "norm" else None
              for x, kind in zip(xs, pattern)]
        for hd, kind in enumerate(pattern):
            if kind == "norm":
                y = rotary(xs[hd] * g, cs, ss) * rs[hd]
            elif kind == "rope":
                y = rotary(xs[hd], cs, ss)
            else:
                y = xs[hd]
            o_ref[rows, hd * HEAD_DIM:(hd + 1) * HEAD_DIM] = y.astype(BF16)

    def tiles(pattern):
        sub = tm // PROJ_SUBTILES
        spans = [pl.ds(r * sub, sub) for r in range(PROJ_SUBTILES)]
        dot = lambda rows: jnp.dot(n_ref[rows, :], wbf_ref[...], preferred_element_type=F32)
        acc_next = dot(spans[0])
        for r, rows in enumerate(spans):
            acc = acc_next
            if r + 1 < len(spans):
                acc_next = dot(spans[r + 1])
            epilogue(pattern, acc, rows)

    needed = True
    if ctx_skip is not None:
        needed = (pl.program_id(1) < ctx_skip[0]) | (j < ctx_skip[1])

        @pl.when(jnp.logical_not(needed))
        def _():
            o_ref[...] = jnp.zeros(o_ref.shape, BF16)

    for pattern in sorted(set(kinds)):
        cond = functools.reduce(jnp.logical_or, [j == jj for jj, p in enumerate(kinds) if p == pattern])
        pl.when(cond & needed)(functools.partial(tiles, pattern))


def _in_projection(n, w_in, layer, cos, sin, qg, kg, head_kinds, q_col0, tm, ctx_skip_rows):
    rows, d = n.shape
    cols = w_in.shape[2]
    tn = IN_PROJ_TN
    hpt = tn // HEAD_DIM
    kinds = tuple(tuple(head_kinds[j * hpt:(j + 1) * hpt]) for j in range(cols // tn))
    return pl.pallas_call(
        functools.partial(_proj_kernel, kinds, q_col0 // tn,
                          None if ctx_skip_rows is None else (ctx_skip_rows // tm, q_col0 // tn)),
        grid=(cols // tn, rows // tm),
        in_specs=[
            pl.BlockSpec((tm, d), lambda j, i: (i, 0)),
            pl.BlockSpec((None, d, tn), lambda j, i: (layer, 0, j)),
            pl.BlockSpec((tm, HEAD_DIM), lambda j, i: (i, 0)),
            pl.BlockSpec((tm, HEAD_DIM), lambda j, i: (i, 0)),
            pl.BlockSpec((1, HEAD_DIM), lambda j, i: (0, 0)),
            pl.BlockSpec((1, HEAD_DIM), lambda j, i: (0, 0)),
        ],
        out_specs=pl.BlockSpec((tm, tn), lambda j, i: (i, j)),
        out_shape=jax.ShapeDtypeStruct((rows, cols), BF16),
        scratch_shapes=[pltpu.VMEM((d, tn), BF16)],
        compiler_params=_cparams(("arbitrary", "arbitrary"), IN_PROJ_VMEM_LIMIT_BYTES),
        name="in_projection",
    )(n, w_in, cos, sin, qg, kg)


_NT = (((1,), (1,)), ((), ()))
_TN = (((0,), (0,)), ((), ()))


def _online_softmax(score_fns, value_fn, m_ref, l_ref, acc_ref, is_latent, n_lat_keys, lat_chunk):
    def update(g, st, v, first):
        qg = st.shape[1]
        sl = slice(g * qg, (g + 1) * qg)
        m_new = jnp.max(st, axis=0, keepdims=True)
        if not first:
            m_old = m_ref[:, sl]
            m_new = jnp.maximum(m_old, m_new)
            alpha = jnp.exp2(m_old - m_new)
        p = jnp.exp2(st - m_new)
        l_new = jnp.sum(p, axis=0, keepdims=True)
        acc_new = lax.dot_general(v, p.astype(BF16), _TN, preferred_element_type=F32)
        if not first:
            l_new += alpha * l_ref[:, sl]
            acc_new += alpha * acc_ref[:, sl]
        m_ref[:, sl] = m_new
        l_ref[:, sl] = l_new
        acc_ref[:, sl] = acc_new

    def sweep(chunks):
        steps = [(c, g) for c in range(len(chunks)) for g in range(len(score_fns))]
        st_next = score_fns[0](*chunks[0])
        for i, (c, g) in enumerate(steps):
            st = st_next
            if i + 1 < len(steps):
                cn, gn = steps[i + 1]
                st_next = score_fns[gn](*chunks[cn])
            update(g, st, value_fn(*chunks[c]), c == 0)

    ctx_chunk = [("ctx", 0, KV_CHUNK)]
    lat_chunks = [("lat", off, lat_chunk) for off in range(0, n_lat_keys, lat_chunk)]
    if is_latent is True:
        sweep(ctx_chunk + lat_chunks)
    else:
        pl.when(jnp.logical_not(is_latent))(lambda: sweep(ctx_chunk))
        pl.when(is_latent)(lambda: sweep(ctx_chunk + lat_chunks))


def _kv_slicer(ctx_ref, lat_ref, lo=0, width=None):
    width = ctx_ref.shape[1] if width is None else width
    return lambda src, off, size: (ctx_ref if src == "ctx" else lat_ref)[pl.ds(off, size), lo:lo + width]


def _attn_a_kernel(lat_chunk, lat_tiles, with_ctx, q_ref, kc_ref, kl_ref, vc_ref, vl_ref, z_ref, o_ref,
                   m_ref, l_ref, acc_ref):
    tq = q_ref.shape[0]
    q = q_ref[...]
    q4 = jnp.concatenate([q[:, g * HEAD_DIM:(g + 1) * HEAD_DIM] for g in range(A_GROUP)], axis=0)
    keys = _kv_slicer(kc_ref, kl_ref)

    def scores(*chunk):
        return lax.dot_general(keys(*chunk), q4, _NT, preferred_element_type=F32)

    is_latent = (pl.program_id(2) < lat_tiles) if with_ctx else True
    _online_softmax([scores], _kv_slicer(vc_ref, vl_ref), m_ref, l_ref, acc_ref, is_latent,
                    kl_ref.shape[0], lat_chunk)
    o = jnp.transpose(acc_ref[...] * (1.0 / l_ref[...]))
    z = z_ref[...].astype(F32)
    for g in range(A_GROUP):
        og = o[g * tq:(g + 1) * tq] * _silu(z[:, g * HEAD_DIM:(g + 1) * HEAD_DIM])
        o_ref[:, g * HEAD_DIM:(g + 1) * HEAD_DIM] = og.astype(BF16)


def _attn_grid(batch, seq, ctx_len, with_ctx):
    lat_tiles = seq // ROW_TILE
    all_lat = batch * lat_tiles
    n_t = lat_tiles + (1 if with_ctx else 0)
    if with_ctx:
        qrow = lambda b, t: jnp.where(t < lat_tiles, b * lat_tiles + t, all_lat + b)
    else:
        qrow = lambda b, t: b * lat_tiles + t
    out_rows = (all_lat + (batch if with_ctx else 0)) * ROW_TILE
    return lat_tiles, all_lat, n_t, qrow, out_rows


def _attention_a(u, batch, seq, ctx_len, with_ctx, cols):
    ka0, va0, qa0, za0, a_width = cols
    kv_heads = a_width // (A_GROUP * HEAD_DIM)
    tq = ROW_TILE
    qw = A_GROUP * HEAD_DIM
    lat_tiles, all_lat, n_t, qrow, out_rows = _attn_grid(batch, seq, ctx_len, with_ctx)
    ctx_kv = lambda c0: pl.BlockSpec((ctx_len, HEAD_DIM), lambda b, h, t: (all_lat + b, c0 // HEAD_DIM + h))
    lat_kv = lambda c0: pl.BlockSpec((seq, HEAD_DIM), lambda b, h, t: (b, c0 // HEAD_DIM + h))
    return pl.pallas_call(
        functools.partial(_attn_a_kernel, A_LAT_CHUNK, lat_tiles, with_ctx),
        grid=(batch, kv_heads, n_t),
        in_specs=[
            pl.BlockSpec((tq, qw), lambda b, h, t: (qrow(b, t), qa0 // qw + h)),
            ctx_kv(ka0), lat_kv(ka0), ctx_kv(va0), lat_kv(va0),
            pl.BlockSpec((tq, qw), lambda b, h, t: (qrow(b, t), za0 // qw + h)),
        ],
        out_specs=pl.BlockSpec((tq, qw), lambda b, h, t: (qrow(b, t), h)),
        out_shape=jax.ShapeDtypeStruct((out_rows, a_width), BF16),
        scratch_shapes=[pltpu.VMEM((1, A_GROUP * tq), F32), pltpu.VMEM((1, A_GROUP * tq), F32),
                        pltpu.VMEM((HEAD_DIM, A_GROUP * tq), F32)],
        compiler_params=_cparams(("arbitrary", "arbitrary", "arbitrary")),
        name="attention_a",
    )(u, u, u, u, u, u)


def _attn_c_kernel(lat_chunk, lat_tiles, with_ctx, one_minus_lambda_init, lambda_init,
                   q_ref, kc_ref, kl_ref, vc_ref, vl_ref, z_ref, lq1_ref, lk1_ref, lq2_ref, lk2_ref,
                   sg_ref, o_ref, m_ref, l_ref, acc_ref):
    tq = q_ref.shape[0]
    q = q_ref[...]

    def map_scores(lo):
        keys = _kv_slicer(kc_ref, kl_ref, lo, HEAD_DIM)
        qc = q[:, lo:lo + HEAD_DIM]
        return lambda *chunk: lax.dot_general(keys(*chunk), qc, _NT, preferred_element_type=F32)

    is_latent = (pl.program_id(2) < lat_tiles) if with_ctx else True
    _online_softmax([map_scores(0), map_scores(HEAD_DIM)], _kv_slicer(vc_ref, vl_ref), m_ref, l_ref,
                    acc_ref, is_latent, kl_ref.shape[0], lat_chunk)
    lam = (jnp.exp(jnp.sum(lq1_ref[...] * lk1_ref[...], axis=-1, keepdims=True)) -
           jnp.exp(jnp.sum(lq2_ref[...] * lk2_ref[...], axis=-1, keepdims=True)) + lambda_init)
    ot = acc_ref[...] * (1.0 / l_ref[...])
    od = jnp.transpose(ot[:, :tq] - lam * ot[:, tq:])
    y = od * lax.rsqrt(jnp.mean(od * od, axis=-1, keepdims=True) + EPS) * sg_ref[...]
    y = y * one_minus_lambda_init * _silu(z_ref[...].astype(F32))
    o_ref[...] = y.astype(BF16)


def _attention_c(u, lam_vecs, subln_g, lambda_init, batch, seq, ctx_len, with_ctx, cols):
    kc0, vc0, qc0, zc0, c_width = cols
    heads = c_width // C_VDIM
    tq = ROW_TILE
    lat_tiles, all_lat, n_t, qrow, out_rows = _attn_grid(batch, seq, ctx_len, with_ctx)
    vec = pl.BlockSpec((1, HEAD_DIM), lambda b, h, t: (0, 0))
    ctx_kv = lambda c0: pl.BlockSpec((ctx_len, C_VDIM), lambda b, h, t: (all_lat + b, c0 // C_VDIM + h))
    lat_kv = lambda c0: pl.BlockSpec((seq, C_VDIM), lambda b, h, t: (b, c0 // C_VDIM + h))
    return pl.pallas_call(
        functools.partial(_attn_c_kernel, C_LAT_CHUNK, lat_tiles, with_ctx, 1.0 - lambda_init, lambda_init),
        grid=(batch, heads, n_t),
        in_specs=[
            pl.BlockSpec((tq, C_VDIM), lambda b, h, t: (qrow(b, t), qc0 // C_VDIM + h)),
            ctx_kv(kc0), lat_kv(kc0), ctx_kv(vc0), lat_kv(vc0),
            pl.BlockSpec((tq, C_VDIM), lambda b, h, t: (qrow(b, t), zc0 // C_VDIM + h)),
            vec, vec, vec, vec,
            pl.BlockSpec((1, C_VDIM), lambda b, h, t: (0, 0)),
        ],
        out_specs=pl.BlockSpec((tq, C_VDIM), lambda b, h, t: (qrow(b, t), h)),
        out_shape=jax.ShapeDtypeStruct((out_rows, c_width), BF16),
        scratch_shapes=[pltpu.VMEM((1, 2 * tq), F32), pltpu.VMEM((1, 2 * tq), F32),
                        pltpu.VMEM((C_VDIM, 2 * tq), F32)],
        compiler_params=_cparams(("arbitrary", "arbitrary", "arbitrary")),
        name="attention_c",
    )(u, u, u, u, u, u, *lam_vecs, subln_g)


HALO = 16


def _conv_kernel(lat_tiles, tiles_per_batch, xb_ref, bb_ref, cb_ref, zb_ref, xp_ref, cp_ref, xn_ref, cn_ref,
                 w_ref, o_ref, g_ref):
    tr = xb_ref.shape[0]
    t = pl.program_id(0)
    r = t % tiles_per_batch
    has_prev = ((t < lat_tiles) & (r != 0)).astype(F32)
    has_next = ((t < lat_tiles) & (r != tiles_per_batch - 1)).astype(F32)
    g = cb_ref[...].astype(F32) * xb_ref[...].astype(F32)
    g_ref[0:HALO, :] = cp_ref[...].astype(F32) * xp_ref[...].astype(F32) * has_prev
    g_ref[HALO:HALO + tr, :] = g
    g_ref[HALO + tr:HALO + tr + HALO, :] = cn_ref[...].astype(F32) * xn_ref[...].astype(F32) * has_next
    w = w_ref[...]
    conv = (g_ref[HALO - 1:HALO - 1 + tr, :] * w[0:1] + g * w[1:2] +
            g_ref[HALO + 1:HALO + 1 + tr, :] * w[2:3])
    y = bb_ref[...].astype(F32) * conv * _silu(zb_ref[...].astype(F32))
    o_ref[...] = y.astype(BF16)


def _short_conv(u, conv_w, batch, seq, with_ctx, cols):
    rows = u.shape[0]
    xb0, bb0, cb0, zb0, bw = cols
    tr = ROW_TILE
    lat_tiles = batch * seq // tr
    n_tiles = rows // tr if with_ctx else lat_tiles
    hpt = tr // HALO
    last_halo = rows // HALO - 1
    main = lambda c0: pl.BlockSpec((tr, bw), lambda t: (t, c0 // bw))
    prev = lambda c0: pl.BlockSpec((HALO, bw), lambda t: (jnp.maximum(t * hpt - 1, 0), c0 // bw))
    nxt = lambda c0: pl.BlockSpec((HALO, bw), lambda t: (jnp.minimum((t + 1) * hpt, last_halo), c0 // bw))
    return pl.pallas_call(
        functools.partial(_conv_kernel, lat_tiles, seq // tr),
        grid=(n_tiles,),
        in_specs=[main(xb0), main(bb0), main(cb0), main(zb0),
                  prev(xb0), prev(cb0), nxt(xb0), nxt(cb0),
                  pl.BlockSpec(conv_w.shape, lambda t: (0, 0))],
        out_specs=pl.BlockSpec((tr, bw), lambda t: (t, 0)),
        out_shape=jax.ShapeDtypeStruct((n_tiles * tr, bw), BF16),
        scratch_shapes=[pltpu.VMEM((tr + 2 * HALO, bw), F32)],
        compiler_params=_cparams(("arbitrary",)),
        name="short_conv",
    )(u, u, u, u, u, u, u, u, conv_w)


def _out_kernel(lat_m_tiles, oa_ref, yb_ref, oc_ref, w_ref, gate_ref, *refs):
    o_ref, wbf_ref = refs[-2:]

    @pl.when(pl.program_id(1) == 0)
    def _():
        wbf_ref[...] = w_ref[...].astype(BF16)

    ka = oa_ref.shape[1]
    kb = yb_ref.shape[1]
    tm = o_ref.shape[0]
    sub = tm // PROJ_SUBTILES
    gate = gate_ref[0]

    def dot(rows):
        acc = jnp.dot(oa_ref[rows, :], wbf_ref[0:ka, :], preferred_element_type=F32)
        acc += jnp.dot(yb_ref[rows, :], wbf_ref[ka:ka + kb, :], preferred_element_type=F32)
        return acc + jnp.dot(oc_ref[rows, :], wbf_ref[ka + kb:, :], preferred_element_type=F32)

    def residual(rows):
        if len(refs) == 3:
            return refs[0][rows, :]
        return jnp.where(pl.program_id(1) >= lat_m_tiles, refs[1][rows, :], refs[0][rows, :])

    spans = [pl.ds(r * sub, sub) for r in range(PROJ_SUBTILES)]
    acc_next = dot(spans[0])
    for r, rows in enumerate(spans):
        acc = acc_next
        if r + 1 < len(spans):
            acc_next = dot(spans[r + 1])
        o_ref[rows, :] = residual(rows) + gate * acc


def _out_projection(oa, yb, oc, w_out, layer, gates, residuals, batch, tm):
    rows = oa.shape[0]
    d = w_out.shape[2]
    tn = PROJ_TN
    kdim = w_out.shape[1]
    lat_rows = residuals[0].shape[0] if len(residuals) == 2 else batch * (rows // tm // batch) * tm
    lat_m = lat_rows // tm
    grow = lambda j, i: (jnp.where(i >= lat_m, batch, i // (lat_m // batch)), 0, j)
    if len(residuals) == 2:
        res_specs = [pl.BlockSpec((tm, tn), lambda j, i: (jnp.minimum(i, lat_m - 1), j)),
                     pl.BlockSpec((tm, tn), lambda j, i: (jnp.maximum(i - lat_m, 0), j))]
    else:
        res_specs = [pl.BlockSpec((tm, tn), lambda j, i: (i, j))]
    return pl.pallas_call(
        functools.partial(_out_kernel, lat_m),
        grid=(d // tn, rows // tm),
        in_specs=[
            pl.BlockSpec((tm, oa.shape[1]), lambda j, i: (i, 0)),
            pl.BlockSpec((tm, yb.shape[1]), lambda j, i: (i, 0)),
            pl.BlockSpec((tm, oc.shape[1]), lambda j, i: (i, 0)),
            pl.BlockSpec((None, kdim, tn), lambda j, i: (layer, 0, j)),
            pl.BlockSpec((1, 1, tn), grow),
        ] + res_specs,
        out_specs=pl.BlockSpec((tm, tn), lambda j, i: (i, j)),
        out_shape=jax.ShapeDtypeStruct((rows, d), F32),
        scratch_shapes=[pltpu.VMEM((kdim, tn), BF16)],
        compiler_params=_cparams(("arbitrary", "arbitrary")),
        name="out_projection",
    )(oa, yb, oc, w_out, gates, *residuals)


def _final_kernel(h_ref, g_ref, o_ref):
    h = h_ref[...]
    ms = jnp.mean(h * h, axis=-1, keepdims=True)
    o_ref[...] = (h * lax.rsqrt(ms + EPS)) * g_ref[...]


def _final_norm(h, g):
    rows, d = h.shape
    return pl.pallas_call(
        _final_kernel,
        grid=(rows // ROW_TILE,),
        in_specs=[pl.BlockSpec((ROW_TILE, d), lambda t: (t, 0)), pl.BlockSpec((1, d), lambda t: (0, 0))],
        out_specs=pl.BlockSpec((ROW_TILE, d), lambda t: (t, 0)),
        out_shape=jax.ShapeDtypeStruct((rows, d), F32),
        compiler_params=_cparams(("arbitrary",)),
        name="final_norm",
    )(h, g)


def _rope_tables(batch, seq, ctx_len):
    pos = jnp.arange(seq)
    inv = ROPE_THETA ** (-jnp.arange(0, AXIS_DIM, 2, dtype=F32) / AXIS_DIM)
    ar = (pos // GRID_W).astype(F32)[:, None] * inv
    ac = (pos % GRID_W).astype(F32)[:, None] * inv
    cos = jnp.concatenate([jnp.cos(ar), jnp.cos(ar), jnp.cos(ac), jnp.cos(ac)], axis=-1)
    sin = jnp.concatenate([-jnp.sin(ar), jnp.sin(ar), -jnp.sin(ac), jnp.sin(ac)], axis=-1)
    cos = jnp.concatenate([jnp.tile(cos, (batch, 1)), jnp.ones((batch * ctx_len, HEAD_DIM), F32)], axis=0)
    sin = jnp.concatenate([jnp.tile(sin, (batch, 1)), jnp.zeros((batch * ctx_len, HEAD_DIM), F32)], axis=0)
    return cos, sin


def kernel(x, c, ctx, c_ctx, w_mod, b_mod, norm_g, w_in, q_norm_a, k_norm_a, conv_w, lambda_q1,
           lambda_k1, lambda_q2, lambda_k2, subln_g, w_out, final_g):
    batch, seq, d = x.shape
    ctx_len = ctx.shape[1]
    depth = w_in.shape[0]
    lat_rows, ctx_rows = batch * seq, batch * ctx_len
    assert ctx_len == ROW_TILE == KV_CHUNK and seq % ROW_TILE == 0
    assert lat_rows % PROJ_TM == 0 and ctx_rows % PROJ_TM == 0 and (lat_rows // PROJ_TM) % batch == 0

    a_width, b_width, c_width = d // 2, d // 4, d // 4
    kv_a = a_width // A_GROUP
    ka0 = 0
    va0 = ka0 + kv_a
    kc0 = va0 + kv_a
    vc0 = kc0 + c_width
    qa0 = vc0 + c_width
    qc0 = qa0 + a_width
    xb0 = qc0 + c_width
    bb0 = xb0 + b_width
    cb0 = bb0 + b_width
    za0 = cb0 + b_width
    zb0 = za0 + a_width
    zc0 = zb0 + b_width
    assert zc0 + c_width == w_in.shape[2]
    head_kind = lambda c: ("norm" if c < va0 or qa0 <= c < qc0 else
                           "rope" if kc0 <= c < vc0 or qc0 <= c < xb0 else "plain")
    head_kinds = [head_kind(c) for c in range(0, w_in.shape[2], HEAD_DIM)]
    assert qa0 % IN_PROJ_TN == 0

    cvecs = jnp.concatenate([c, c_ctx[None, :], jnp.zeros((8 - batch - 1, d), F32)], axis=0)
    mods = _modulation(cvecs, w_mod, b_mod)
    cos, sin = _rope_tables(batch, seq, ctx_len)
    x2 = x.reshape(lat_rows, d)
    c2 = ctx.reshape(ctx_rows, d)

    h = None
    for i in range(depth):
        update_ctx = i < depth - 1
        lambda_init = 0.8 - 0.6 * math.exp(-0.3 * i)
        shift = mods[i, :, 0:d].reshape(8, 1, d)
        scale = mods[i, :, d:2 * d].reshape(8, 1, d)
        gates = mods[i, :, 2 * d:3 * d].reshape(8, 1, d)
        n = _adaln_norm(x2, c2, h, norm_g[i].reshape(1, d), scale, shift, batch)
        u = _in_projection(n, w_in, i, cos, sin, q_norm_a[i].reshape(1, HEAD_DIM),
                           k_norm_a[i].reshape(1, HEAD_DIM), head_kinds, qa0, PROJ_TM,
                           None if update_ctx else lat_rows)
        oa = _attention_a(u, batch, seq, ctx_len, update_ctx, (ka0, va0, qa0, za0, a_width))
        lam_vecs = [v[i].reshape(1, HEAD_DIM) for v in (lambda_q1, lambda_k1, lambda_q2, lambda_k2)]
        oc = _attention_c(u, lam_vecs, subln_g[i].reshape(1, C_VDIM), lambda_init, batch, seq, ctx_len,
                          update_ctx, (kc0, vc0, qc0, zc0, c_width))
        yb = _short_conv(u, conv_w[i], batch, seq, update_ctx, (xb0, bb0, cb0, zb0, b_width))
        h = _out_projection(oa, yb, oc, w_out, i, gates, (x2, c2) if h is None else (h,), batch, PROJ_TM)
    return _final_norm(h, final_g.reshape(1, d)).reshape(batch, seq, d)
```

```python
import functools
import math

import jax
import jax.numpy as jnp
from jax import lax
from jax.experimental import pallas as pl
from jax.experimental.pallas import tpu as pltpu

GRID_W = 64
HEAD_DIM = 128
AXIS_DIM = HEAD_DIM // 2
ROPE_THETA = 10000.0
EPS = 1e-6
ATTN_SCALE = 1.0 / math.sqrt(HEAD_DIM)
Q_PRESCALE = ATTN_SCALE * math.log2(math.e)
A_GROUP = 4
C_VDIM = 2 * HEAD_DIM

F32 = jnp.float32
BF16 = jnp.bfloat16

VMEM_LIMIT_BYTES = 56 * 1024 * 1024
IN_PROJ_VMEM_LIMIT_BYTES = 61 * 1024 * 1024
ROW_TILE = 256
KV_CHUNK = 256
A_LAT_CHUNK = 1024
C_LAT_CHUNK = 256
PROJ_TN = 512
IN_PROJ_TN = 1024
PROJ_TM = 512
OUT_PROJ_LATENT_TM = 1024
PROJ_SUBTILES = 4
ATTN_SUBTILES = 2


def _cparams(sem, vmem_limit_bytes=VMEM_LIMIT_BYTES):
    return pltpu.CompilerParams(dimension_semantics=sem, vmem_limit_bytes=vmem_limit_bytes)


def _silu(x):
    return x * (1.0 / (1.0 + jnp.exp(-x)))


def _mod_kernel(c_ref, w_ref, b_ref, o_ref):
    s = _silu(c_ref[...]).astype(BF16)
    w = w_ref[0].astype(BF16)
    o_ref[0] = jnp.dot(s, w, preferred_element_type=F32) + b_ref[0]


def _modulation(cvecs, w_mod, b_mod):
    depth, d, n3 = w_mod.shape
    tn = PROJ_TN
    return pl.pallas_call(
        _mod_kernel,
        grid=(depth, n3 // tn),
        in_specs=[
            pl.BlockSpec((8, d), lambda l, j: (0, 0)),
            pl.BlockSpec((1, d, tn), lambda l, j: (l, 0, j)),
            pl.BlockSpec((1, 1, tn), lambda l, j: (l, 0, j)),
        ],
        out_specs=pl.BlockSpec((1, 8, tn), lambda l, j: (l, 0, j)),
        out_shape=jax.ShapeDtypeStruct((depth, 8, n3), F32),
        compiler_params=_cparams(("arbitrary", "arbitrary")),
        name="modulation",
    )(cvecs, w_mod, b_mod.reshape(depth, 1, n3))


def _norm_math(h, g, scale, shift):
    ms = jnp.mean(h * h, axis=-1, keepdims=True)
    return ((h * lax.rsqrt(ms + EPS)) * g * (1.0 + scale) + shift).astype(BF16)


def _norm0_kernel(lat_tiles, x_ref, c_ref, g_ref, sc_ref, sh_ref, n_ref):
    h = jnp.where(pl.program_id(0) >= lat_tiles, c_ref[...], x_ref[...])
    n_ref[...] = _norm_math(h, g_ref[...], sc_ref[0], sh_ref[0])


def _norm1_kernel(h_ref, g_ref, sc_ref, sh_ref, n_ref):
    n_ref[...] = _norm_math(h_ref[...], g_ref[...], sc_ref[0], sh_ref[0])


def _mod_row(t, lat_tiles, tiles_per_batch, ctx_row):
    return jnp.where(t >= lat_tiles, ctx_row, t // tiles_per_batch)


def _adaln_norm(x2, c2, h, g, scale, shift, batch):
    first = h is None
    d = x2.shape[1]
    lat_tiles = x2.shape[0] // ROW_TILE
    n_tiles = lat_tiles + c2.shape[0] // ROW_TILE
    mrow = lambda t: (_mod_row(t, lat_tiles, lat_tiles // batch, batch), 0, 0)
    tile = lambda imap: pl.BlockSpec((ROW_TILE, d), imap)
    if first:
        body = functools.partial(_norm0_kernel, lat_tiles)
        acts = [x2, c2]
        act_specs = [tile(lambda t: (jnp.minimum(t, lat_tiles - 1), 0)),
                     tile(lambda t: (jnp.maximum(t - lat_tiles, 0), 0))]
    else:
        body, acts, act_specs = _norm1_kernel, [h], [tile(lambda t: (t, 0))]
    return pl.pallas_call(
        body,
        grid=(n_tiles,),
        in_specs=act_specs + [pl.BlockSpec((1, d), lambda t: (0, 0)),
                              pl.BlockSpec((1, 1, d), mrow), pl.BlockSpec((1, 1, d), mrow)],
        out_specs=tile(lambda t: (t, 0)),
        out_shape=jax.ShapeDtypeStruct((n_tiles * ROW_TILE, d), BF16),
        compiler_params=_cparams(("arbitrary",)),
        name="adaln_norm",
    )(*acts, g, scale, shift)


def _rope_partner(x):
    lane = lax.broadcasted_iota(jnp.int32, x.shape, 1)
    fwd = pltpu.roll(x, HEAD_DIM - AXIS_DIM // 2, axis=1)
    bwd = pltpu.roll(x, AXIS_DIM // 2, axis=1)
    return jnp.where((lane % AXIS_DIM) < AXIS_DIM // 2, fwd, bwd)


def _proj_kernel(kinds, q_tile0, ctx_skip, n_ref, w_ref, cos_ref, sin_ref, qg_ref, kg_ref, o_ref, wbf_ref):
    j = pl.program_id(0)

    @pl.when(pl.program_id(1) == 0)
    def _():
        wbf_ref[...] = w_ref[...].astype(BF16)

    tm, tn = o_ref.shape
    is_q = j >= q_tile0
    qscale = jnp.where(is_q, Q_PRESCALE, 1.0).astype(F32)

    def rotary(x, cs, ss):
        return x * cs + _rope_partner(x) * ss

    def epilogue(pattern, acc, rows):
        if all(kind == "plain" for kind in pattern):
            o_ref[rows, :] = acc.astype(BF16)
            return
        g = jnp.where(is_q, qg_ref[...], kg_ref[...])
        cs = cos_ref[rows, :] * qscale
        ss = sin_ref[rows, :] * qscale
        xs = [acc[:, hd * HEAD_DIM:(hd + 1) * HEAD_DIM] for hd in range(len(pattern))]
        rs = [lax.rsqrt(jnp.mean(x * x, axis=-1, keepdims=True) + EPS) if kind == "norm" else None
              for x, kind in zip(xs, pattern)]
        for hd, kind in enumerate(pattern):
            if kind == "norm":
                y = rotary(xs[hd] * g, cs, ss) * rs[hd]
            elif kind == "rope":
                y = rotary(xs[hd], cs, ss)
            else:
                y = xs[hd]
            o_ref[rows, hd * HEAD_DIM:(hd + 1) * HEAD_DIM] = y.astype(BF16)

    def tiles(pattern):
        sub = tm // PROJ_SUBTILES
        spans = [pl.ds(r * sub, sub) for r in range(PROJ_SUBTILES)]
        dot = lambda rows: jnp.dot(n_ref[rows, :], wbf_ref[...], preferred_element_type=F32)
        acc_next = dot(spans[0])
        for r, rows in enumerate(spans):
            acc = acc_next
            if r + 1 < len(spans):
                acc_next = dot(spans[r + 1])
            epilogue(pattern, acc, rows)

    needed = True
    if ctx_skip is not None:
        needed = (pl.program_id(1) < ctx_skip[0]) | (j < ctx_skip[1])

        @pl.when(jnp.logical_not(needed))
        def _():
            o_ref[...] = jnp.zeros(o_ref.shape, BF16)

    for pattern in sorted(set(kinds)):
        cond = functools.reduce(jnp.logical_or, [j == jj for jj, p in enumerate(kinds) if p == pattern])
        pl.when(cond & needed)(functools.partial(tiles, pattern))


def _in_projection(n, w_in, layer, cos, sin, qg, kg, head_kinds, q_col0, tm, ctx_skip_rows):
    rows, d = n.shape
    cols = w_in.shape[2]
    tn = IN_PROJ_TN
    hpt = tn // HEAD_DIM
    kinds = tuple(tuple(head_kinds[j * hpt:(j + 1) * hpt]) for j in range(cols // tn))
    return pl.pallas_call(
        functools.partial(_proj_kernel, kinds, q_col0 // tn,
                          None if ctx_skip_rows is None else (ctx_skip_rows // tm, q_col0 // tn)),
        grid=(cols // tn, rows // tm),
        in_specs=[
            pl.BlockSpec((tm, d), lambda j, i: (i, 0)),
            pl.BlockSpec((None, d, tn), lambda j, i: (layer, 0, j)),
            pl.BlockSpec((tm, HEAD_DIM), lambda j, i: (i, 0)),
            pl.BlockSpec((tm, HEAD_DIM), lambda j, i: (i, 0)),
            pl.BlockSpec((1, HEAD_DIM), lambda j, i: (0, 0)),
            pl.BlockSpec((1, HEAD_DIM), lambda j, i: (0, 0)),
        ],
        out_specs=pl.BlockSpec((tm, tn), lambda j, i: (i, j)),
        out_shape=jax.ShapeDtypeStruct((rows, cols), BF16),
        scratch_shapes=[pltpu.VMEM((d, tn), BF16)],
        compiler_params=_cparams(("arbitrary", "arbitrary"), IN_PROJ_VMEM_LIMIT_BYTES),
        name="in_projection",
    )(n, w_in, cos, sin, qg, kg)


_NT = (((1,), (1,)), ((), ()))
_TN = (((0,), (0,)), ((), ()))


def _softmax_pipeline(items, m_ref, l_ref, acc_ref):
    st_next = items[0][0]()
    for i, (_, value, lanes, first, done) in enumerate(items):
        st = st_next
        if i + 1 < len(items):
            st_next = items[i + 1][0]()
        m_new = jnp.max(st, axis=0, keepdims=True)
        if not first:
            m_old = m_ref[:, lanes]
            m_new = jnp.maximum(m_old, m_new)
            alpha = jnp.exp2(m_old - m_new)
        p = jnp.exp2(st - m_new)
        l_new = jnp.sum(p, axis=0, keepdims=True)
        acc_new = lax.dot_general(value(), p.astype(BF16), _TN, preferred_element_type=F32)
        if not first:
            l_new += alpha * l_ref[:, lanes]
            acc_new += alpha * acc_ref[:, lanes]
        m_ref[:, lanes] = m_new
        l_ref[:, lanes] = l_new
        acc_ref[:, lanes] = acc_new
        if done is not None:
            done()


def _key_chunks(kc_ref, kl_ref, lat_chunk):
    chunks = [(kc_ref, 0, kc_ref.shape[0])]
    if kl_ref is not None:
        chunks += [(kl_ref, off, lat_chunk) for off in range(0, kl_ref.shape[0], lat_chunk)]
    return chunks


def _attn_a_kernel(lat_chunk, latent, *refs):
    if latent:
        q_ref, kc_ref, kl_ref, vc_ref, vl_ref, z_ref, o_ref, m_ref, l_ref, acc_ref = refs
    else:
        q_ref, kc_ref, vc_ref, z_ref, o_ref, m_ref, l_ref, acc_ref = refs
        kl_ref = vl_ref = None
    tq = ROW_TILE
    lanes_per_tile = A_GROUP * tq
    items = []
    for s in range(q_ref.shape[0] // tq):
        rows = slice(s * tq, (s + 1) * tq)
        lanes = slice(s * lanes_per_tile, (s + 1) * lanes_per_tile)
        q4 = jnp.concatenate([q_ref[rows, g * HEAD_DIM:(g + 1) * HEAD_DIM] for g in range(A_GROUP)], axis=0)

        def finalize(rows=rows, lanes=lanes):
            o = jnp.transpose(acc_ref[:, lanes] * (1.0 / l_ref[:, lanes]))
            z = z_ref[rows, :].astype(F32)
            for g in range(A_GROUP):
                og = o[g * tq:(g + 1) * tq] * _silu(z[:, g * HEAD_DIM:(g + 1) * HEAD_DIM])
                o_ref[rows, g * HEAD_DIM:(g + 1) * HEAD_DIM] = og.astype(BF16)

        k_chunks = _key_chunks(kc_ref, kl_ref, lat_chunk)
        v_chunks = _key_chunks(vc_ref, vl_ref, lat_chunk)
        for c, ((k_ref, off, size), (v_ref, _, _)) in enumerate(zip(k_chunks, v_chunks)):
            score = lambda k_ref=k_ref, off=off, size=size, q4=q4: lax.dot_general(
                k_ref[pl.ds(off, size), :], q4, _NT, preferred_element_type=F32)
            value = lambda v_ref=v_ref, off=off, size=size: v_ref[pl.ds(off, size), :]
            items.append((score, value, lanes, c == 0, finalize if c == len(k_chunks) - 1 else None))
    _softmax_pipeline(items, m_ref, l_ref, acc_ref)


def _attention_a(u, batch, seq, ctx_len, with_ctx, cols):
    ka0, va0, qa0, za0, a_width = cols
    kv_heads = a_width // (A_GROUP * HEAD_DIM)
    qw = A_GROUP * HEAD_DIM
    tq = ATTN_SUBTILES * ROW_TILE
    lat_steps = seq // tq
    lat_rows = batch * seq
    ctx_row = lat_rows // ctx_len
    ctx_kv = lambda c0: pl.BlockSpec((ctx_len, HEAD_DIM), lambda b, h, *t: (ctx_row + b, c0 // HEAD_DIM + h))
    lat_kv = lambda c0: pl.BlockSpec((seq, HEAD_DIM), lambda b, h, t: (b, c0 // HEAD_DIM + h))
    scratch = lambda tiles: [pltpu.VMEM((1, tiles * A_GROUP * ROW_TILE), F32),
                             pltpu.VMEM((1, tiles * A_GROUP * ROW_TILE), F32),
                             pltpu.VMEM((HEAD_DIM, tiles * A_GROUP * ROW_TILE), F32)]
    oa = pl.pallas_call(
        functools.partial(_attn_a_kernel, A_LAT_CHUNK, True),
        grid=(batch, kv_heads, lat_steps),
        in_specs=[
            pl.BlockSpec((tq, qw), lambda b, h, t: (b * lat_steps + t, qa0 // qw + h)),
            ctx_kv(ka0), lat_kv(ka0), ctx_kv(va0), lat_kv(va0),
            pl.BlockSpec((tq, qw), lambda b, h, t: (b * lat_steps + t, za0 // qw + h)),
        ],
        out_specs=pl.BlockSpec((tq, qw), lambda b, h, t: (b * lat_steps + t, h)),
        out_shape=jax.ShapeDtypeStruct((lat_rows, a_width), BF16),
        scratch_shapes=scratch(ATTN_SUBTILES),
        compiler_params=_cparams(("arbitrary", "arbitrary", "arbitrary")),
        name="attention_a",
    )(u, u, u, u, u, u)
    if not with_ctx:
        return oa, None
    oa_ctx = pl.pallas_call(
        functools.partial(_attn_a_kernel, A_LAT_CHUNK, False),
        grid=(batch, kv_heads),
        in_specs=[
            pl.BlockSpec((ctx_len, qw), lambda b, h: (ctx_row + b, qa0 // qw + h)),
            ctx_kv(ka0), ctx_kv(va0),
            pl.BlockSpec((ctx_len, qw), lambda b, h: (ctx_row + b, za0 // qw + h)),
        ],
        out_specs=pl.BlockSpec((ctx_len, qw), lambda b, h: (b, h)),
        out_shape=jax.ShapeDtypeStruct((batch * ctx_len, a_width), BF16),
        scratch_shapes=scratch(1),
        compiler_params=_cparams(("arbitrary", "arbitrary")),
        name="attention_a_ctx",
    )(u, u, u, u)
    return oa, oa_ctx


def _attn_c_kernel(lat_chunk, latent, one_minus_lambda_init, lambda_init, *refs):
    if latent:
        (q_ref, kc_ref, kl_ref, vc_ref, vl_ref, z_ref, lq1_ref, lk1_ref, lq2_ref, lk2_ref, sg_ref,
         o_ref, m_ref, l_ref, acc_ref) = refs
    else:
        (q_ref, kc_ref, vc_ref, z_ref, lq1_ref, lk1_ref, lq2_ref, lk2_ref, sg_ref,
         o_ref, m_ref, l_ref, acc_ref) = refs
        kl_ref = vl_ref = None
    tq = ROW_TILE
    lam = (jnp.exp(jnp.sum(lq1_ref[...] * lk1_ref[...], axis=-1, keepdims=True)) -
           jnp.exp(jnp.sum(lq2_ref[...] * lk2_ref[...], axis=-1, keepdims=True)) + lambda_init)
    items = []
    for s in range(q_ref.shape[0] // tq):
        rows = slice(s * tq, (s + 1) * tq)
        lane0 = 2 * s * tq

        def finalize(rows=rows, lane0=lane0):
            lanes = slice(lane0, lane0 + 2 * tq)
            ot = acc_ref[:, lanes] * (1.0 / l_ref[:, lanes])
            od = jnp.transpose(ot[:, :tq] - lam * ot[:, tq:])
            y = od * lax.rsqrt(jnp.mean(od * od, axis=-1, keepdims=True) + EPS) * sg_ref[...]
            y = y * one_minus_lambda_init * _silu(z_ref[rows, :].astype(F32))
            o_ref[rows, :] = y.astype(BF16)

        k_chunks = _key_chunks(kc_ref, kl_ref, lat_chunk)
        v_chunks = _key_chunks(vc_ref, vl_ref, lat_chunk)
        for c, ((k_ref, off, size), (v_ref, _, _)) in enumerate(zip(k_chunks, v_chunks)):
            for g in range(2):
                lo = g * HEAD_DIM
                qc = q_ref[rows, lo:lo + HEAD_DIM]
                score = lambda k_ref=k_ref, off=off, size=size, lo=lo, qc=qc: lax.dot_general(
                    k_ref[pl.ds(off, size), lo:lo + HEAD_DIM], qc, _NT, preferred_element_type=F32)
                value = lambda v_ref=v_ref, off=off, size=size: v_ref[pl.ds(off, size), :]
                last = c == len(k_chunks) - 1 and g == 1
                items.append((score, value, slice(lane0 + g * tq, lane0 + (g + 1) * tq), c == 0,
                              finalize if last else None))
    _softmax_pipeline(items, m_ref, l_ref, acc_ref)


def _attention_c(u, lam_vecs, subln_g, lambda_init, batch, seq, ctx_len, with_ctx, cols):
    kc0, vc0, qc0, zc0, c_width = cols
    heads = c_width // C_VDIM
    tq = ATTN_SUBTILES * ROW_TILE
    lat_steps = seq // tq
    lat_rows = batch * seq
    ctx_row = lat_rows // ctx_len
    vec = lambda n: pl.BlockSpec((1, n), lambda b, h, *t: (0, 0))
    params = [vec(HEAD_DIM)] * 4 + [vec(C_VDIM)]
    ctx_kv = lambda c0: pl.BlockSpec((ctx_len, C_VDIM), lambda b, h, *t: (ctx_row + b, c0 // C_VDIM + h))
    lat_kv = lambda c0: pl.BlockSpec((seq, C_VDIM), lambda b, h, t: (b, c0 // C_VDIM + h))
    scratch = lambda tiles: [pltpu.VMEM((1, tiles * 2 * ROW_TILE), F32), pltpu.VMEM((1, tiles * 2 * ROW_TILE), F32),
                             pltpu.VMEM((C_VDIM, tiles * 2 * ROW_TILE), F32)]
    oc = pl.pallas_call(
        functools.partial(_attn_c_kernel, C_LAT_CHUNK, True, 1.0 - lambda_init, lambda_init),
        grid=(batch, heads, lat_steps),
        in_specs=[
            pl.BlockSpec((tq, C_VDIM), lambda b, h, t: (b * lat_steps + t, qc0 // C_VDIM + h)),
            ctx_kv(kc0), lat_kv(kc0), ctx_kv(vc0), lat_kv(vc0),
            pl.BlockSpec((tq, C_VDIM), lambda b, h, t: (b * lat_steps + t, zc0 // C_VDIM + h)),
        ] + params,
        out_specs=pl.BlockSpec((tq, C_VDIM), lambda b, h, t: (b * lat_steps + t, h)),
        out_shape=jax.ShapeDtypeStruct((lat_rows, c_width), BF16),
        scratch_shapes=scratch(ATTN_SUBTILES),
        compiler_params=_cparams(("arbitrary", "arbitrary", "arbitrary")),
        name="attention_c",
    )(u, u, u, u, u, u, *lam_vecs, subln_g)
    if not with_ctx:
        return oc, None
    oc_ctx = pl.pallas_call(
        functools.partial(_attn_c_kernel, C_LAT_CHUNK, False, 1.0 - lambda_init, lambda_init),
        grid=(batch, heads),
        in_specs=[
            pl.BlockSpec((ctx_len, C_VDIM), lambda b, h: (ctx_row + b, qc0 // C_VDIM + h)),
            ctx_kv(kc0), ctx_kv(vc0),
            pl.BlockSpec((ctx_len, C_VDIM), lambda b, h: (ctx_row + b, zc0 // C_VDIM + h)),
        ] + params,
        out_specs=pl.BlockSpec((ctx_len, C_VDIM), lambda b, h: (b, h)),
        out_shape=jax.ShapeDtypeStruct((batch * ctx_len, c_width), BF16),
        scratch_shapes=scratch(1),
        compiler_params=_cparams(("arbitrary", "arbitrary")),
        name="attention_c_ctx",
    )(u, u, u, u, *lam_vecs, subln_g)
    return oc, oc_ctx


HALO = 16


def _conv_kernel(lat_tiles, tiles_per_batch, xb_ref, bb_ref, cb_ref, zb_ref, xp_ref, cp_ref, xn_ref, cn_ref,
                 w_ref, o_ref, g_ref):
    tr = xb_ref.shape[0]
    t = pl.program_id(0)
    r = t % tiles_per_batch
    has_prev = ((t < lat_tiles) & (r != 0)).astype(F32)
    has_next = ((t < lat_tiles) & (r != tiles_per_batch - 1)).astype(F32)
    g = cb_ref[...].astype(F32) * xb_ref[...].astype(F32)
    g_ref[0:HALO, :] = cp_ref[...].astype(F32) * xp_ref[...].astype(F32) * has_prev
    g_ref[HALO:HALO + tr, :] = g
    g_ref[HALO + tr:HALO + tr + HALO, :] = cn_ref[...].astype(F32) * xn_ref[...].astype(F32) * has_next
    w = w_ref[...]
    conv = (g_ref[HALO - 1:HALO - 1 + tr, :] * w[0:1] + g * w[1:2] +
            g_ref[HALO + 1:HALO + 1 + tr, :] * w[2:3])
    y = bb_ref[...].astype(F32) * conv * _silu(zb_ref[...].astype(F32))
    o_ref[...] = y.astype(BF16)


def _short_conv(u, conv_w, batch, seq, with_ctx, cols):
    rows = u.shape[0]
    xb0, bb0, cb0, zb0, bw = cols
    tr = ROW_TILE
    lat_tiles = batch * seq // tr
    n_tiles = rows // tr if with_ctx else lat_tiles
    hpt = tr // HALO
    last_halo = rows // HALO - 1
    main = lambda c0: pl.BlockSpec((tr, bw), lambda t: (t, c0 // bw))
    prev = lambda c0: pl.BlockSpec((HALO, bw), lambda t: (jnp.maximum(t * hpt - 1, 0), c0 // bw))
    nxt = lambda c0: pl.BlockSpec((HALO, bw), lambda t: (jnp.minimum((t + 1) * hpt, last_halo), c0 // bw))
    return pl.pallas_call(
        functools.partial(_conv_kernel, lat_tiles, seq // tr),
        grid=(n_tiles,),
        in_specs=[main(xb0), main(bb0), main(cb0), main(zb0),
                  prev(xb0), prev(cb0), nxt(xb0), nxt(cb0),
                  pl.BlockSpec(conv_w.shape, lambda t: (0, 0))],
        out_specs=pl.BlockSpec((tr, bw), lambda t: (t, 0)),
        out_shape=jax.ShapeDtypeStruct((n_tiles * tr, bw), BF16),
        scratch_shapes=[pltpu.VMEM((tr + 2 * HALO, bw), F32)],
        compiler_params=_cparams(("arbitrary",)),
        name="short_conv",
    )(u, u, u, u, u, u, u, u, conv_w)


def _out_kernel(lat_m_tiles, n_streams, *refs):
    streams = [refs[4 * s:4 * s + 4] for s in range(n_streams)]
    w_ref, gate_ref, o_ref, wbf_ref = refs[4 * n_streams:]

    @pl.when(pl.program_id(1) == 0)
    def _():
        wbf_ref[...] = w_ref[...].astype(BF16)

    tm = o_ref.shape[0]
    sub = tm // PROJ_SUBTILES
    gate = gate_ref[0]

    def tile(oa_ref, yb_ref, oc_ref, res_ref):
        ka = oa_ref.shape[1]
        kb = yb_ref.shape[1]

        def dot(rows):
            acc = jnp.dot(oa_ref[rows, :], wbf_ref[0:ka, :], preferred_element_type=F32)
            acc += jnp.dot(yb_ref[rows, :], wbf_ref[ka:ka + kb, :], preferred_element_type=F32)
            return acc + jnp.dot(oc_ref[rows, :], wbf_ref[ka + kb:, :], preferred_element_type=F32)

        spans = [pl.ds(r * sub, sub) for r in range(PROJ_SUBTILES)]
        acc_next = dot(spans[0])
        for r, rows in enumerate(spans):
            acc = acc_next
            if r + 1 < len(spans):
                acc_next = dot(spans[r + 1])
            o_ref[rows, :] = res_ref[rows, :] + gate * acc

    if n_streams == 1:
        tile(*streams[0])
    else:
        is_lat = pl.program_id(1) < lat_m_tiles
        pl.when(is_lat)(lambda: tile(*streams[0]))
        pl.when(jnp.logical_not(is_lat))(lambda: tile(*streams[1]))


def _out_projection(latent, context, w_out, layer, gates, batch, tm):
    d = w_out.shape[2]
    tn = PROJ_TN
    kdim = w_out.shape[1]
    lat_m = latent[0].shape[0] // tm
    ctx_m = 0 if context is None else context[0].shape[0] // tm
    grow = lambda j, i: (jnp.where(i >= lat_m, batch, i // (lat_m // batch)), 0, j)

    def stream_specs(stream, first, count):
        oa, yb, oc, res, yb0, res0 = stream
        own = lambda i: jnp.clip(i - first, 0, count - 1)
        return [pl.BlockSpec((tm, oa.shape[1]), lambda j, i: (own(i), 0)),
                pl.BlockSpec((tm, yb.shape[1]), lambda j, i: (yb0 + own(i), 0)),
                pl.BlockSpec((tm, oc.shape[1]), lambda j, i: (own(i), 0)),
                pl.BlockSpec((tm, tn), lambda j, i: (res0 + own(i), j))]

    specs = stream_specs(latent, 0, lat_m)
    operands = list(latent[:4])
    if context is not None:
        specs += stream_specs(context, lat_m, ctx_m)
        operands += list(context[:4])
    return pl.pallas_call(
        functools.partial(_out_kernel, lat_m, 1 if context is None else 2),
        grid=(d // tn, lat_m + ctx_m),
        in_specs=specs + [pl.BlockSpec((None, kdim, tn), lambda j, i: (layer, 0, j)),
                          pl.BlockSpec((1, 1, tn), grow)],
        out_specs=pl.BlockSpec((tm, tn), lambda j, i: (i, j)),
        out_shape=jax.ShapeDtypeStruct(((lat_m + ctx_m) * tm, d), F32),
        scratch_shapes=[pltpu.VMEM((kdim, tn), BF16)],
        compiler_params=_cparams(("arbitrary", "arbitrary")),
        name="out_projection",
    )(*operands, w_out, gates)


def _final_kernel(h_ref, g_ref, o_ref):
    h = h_ref[...]
    ms = jnp.mean(h * h, axis=-1, keepdims=True)
    o_ref[...] = (h * lax.rsqrt(ms + EPS)) * g_ref[...]


def _final_norm(h, g):
    rows, d = h.shape
    return pl.pallas_call(
        _final_kernel,
        grid=(rows // ROW_TILE,),
        in_specs=[pl.BlockSpec((ROW_TILE, d), lambda t: (t, 0)), pl.BlockSpec((1, d), lambda t: (0, 0))],
        out_specs=pl.BlockSpec((ROW_TILE, d), lambda t: (t, 0)),
        out_shape=jax.ShapeDtypeStruct((rows, d), F32),
        compiler_params=_cparams(("arbitrary",)),
        name="final_norm",
    )(h, g)


def _rope_tables(batch, seq, ctx_len):
    pos = jnp.arange(seq)
    inv = ROPE_THETA ** (-jnp.arange(0, AXIS_DIM, 2, dtype=F32) / AXIS_DIM)
    ar = (pos // GRID_W).astype(F32)[:, None] * inv
    ac = (pos % GRID_W).astype(F32)[:, None] * inv
    cos = jnp.concatenate([jnp.cos(ar), jnp.cos(ar), jnp.cos(ac), jnp.cos(ac)], axis=-1)
    sin = jnp.concatenate([-jnp.sin(ar), jnp.sin(ar), -jnp.sin(ac), jnp.sin(ac)], axis=-1)
    cos = jnp.concatenate([jnp.tile(cos, (batch, 1)), jnp.ones((batch * ctx_len, HEAD_DIM), F32)], axis=0)
    sin = jnp.concatenate([jnp.tile(sin, (batch, 1)), jnp.zeros((batch * ctx_len, HEAD_DIM), F32)], axis=0)
    return cos, sin


def kernel(x, c, ctx, c_ctx, w_mod, b_mod, norm_g, w_in, q_norm_a, k_norm_a, conv_w, lambda_q1,
           lambda_k1, lambda_q2, lambda_k2, subln_g, w_out, final_g):
    batch, seq, d = x.shape
    ctx_len = ctx.shape[1]
    depth = w_in.shape[0]
    lat_rows, ctx_rows = batch * seq, batch * ctx_len
    assert ctx_len == ROW_TILE == KV_CHUNK and seq % ROW_TILE == 0
    assert lat_rows % PROJ_TM == 0 and ctx_rows % PROJ_TM == 0 and (lat_rows // PROJ_TM) % batch == 0

    a_width, b_width, c_width = d // 2, d // 4, d // 4
    kv_a = a_width // A_GROUP
    ka0 = 0
    va0 = ka0 + kv_a
    kc0 = va0 + kv_a
    vc0 = kc0 + c_width
    qa0 = vc0 + c_width
    qc0 = qa0 + a_width
    xb0 = qc0 + c_width
    bb0 = xb0 + b_width
    cb0 = bb0 + b_width
    za0 = cb0 + b_width
    zb0 = za0 + a_width
    zc0 = zb0 + b_width
    assert zc0 + c_width == w_in.shape[2]
    head_kind = lambda c: ("norm" if c < va0 or qa0 <= c < qc0 else
                           "rope" if kc0 <= c < vc0 or qc0 <= c < xb0 else "plain")
    head_kinds = [head_kind(c) for c in range(0, w_in.shape[2], HEAD_DIM)]
    assert qa0 % IN_PROJ_TN == 0

    cvecs = jnp.concatenate([c, c_ctx[None, :], jnp.zeros((8 - batch - 1, d), F32)], axis=0)
    mods = _modulation(cvecs, w_mod, b_mod)
    cos, sin = _rope_tables(batch, seq, ctx_len)
    x2 = x.reshape(lat_rows, d)
    c2 = ctx.reshape(ctx_rows, d)

    h = None
    for i in range(depth):
        update_ctx = i < depth - 1
        lambda_init = 0.8 - 0.6 * math.exp(-0.3 * i)
        shift = mods[i, :, 0:d].reshape(8, 1, d)
        scale = mods[i, :, d:2 * d].reshape(8, 1, d)
        gates = mods[i, :, 2 * d:3 * d].reshape(8, 1, d)
        n = _adaln_norm(x2, c2, h, norm_g[i].reshape(1, d), scale, shift, batch)
        u = _in_projection(n, w_in, i, cos, sin, q_norm_a[i].reshape(1, HEAD_DIM),
                           k_norm_a[i].reshape(1, HEAD_DIM), head_kinds, qa0, PROJ_TM,
                           None if update_ctx else lat_rows)
        oa, oa_ctx = _attention_a(u, batch, seq, ctx_len, update_ctx, (ka0, va0, qa0, za0, a_width))
        lam_vecs = [v[i].reshape(1, HEAD_DIM) for v in (lambda_q1, lambda_k1, lambda_q2, lambda_k2)]
        oc, oc_ctx = _attention_c(u, lam_vecs, subln_g[i].reshape(1, C_VDIM), lambda_init, batch, seq, ctx_len,
                          update_ctx, (kc0, vc0, qc0, zc0, c_width))
        yb = _short_conv(u, conv_w[i], batch, seq, update_ctx, (xb0, bb0, cb0, zb0, b_width))
        tm = PROJ_TM if update_ctx else OUT_PROJ_LATENT_TM
        latent = (oa, yb, oc, x2 if h is None else h, 0, 0)
        context = None
        if update_ctx:
            ctx_tile0 = lat_rows // tm
            context = (oa_ctx, yb, oc_ctx) + ((c2, ctx_tile0, 0) if h is None else (h, ctx_tile0, ctx_tile0))
        h = _out_projection(latent, context, w_out, i, gates, batch, tm)
    return _final_norm(h, final_g.reshape(1, d)).reshape(batch, seq, d)
```

```python
import functools
import math

import jax
import jax.numpy as jnp
from jax import lax
from jax.experimental import pallas as pl
from jax.experimental.pallas import tpu as pltpu

GRID_W = 64
HEAD_DIM = 128
AXIS_DIM = HEAD_DIM // 2
ROPE_THETA = 10000.0
EPS = 1e-6
ATTN_SCALE = 1.0 / math.sqrt(HEAD_DIM)
Q_PRESCALE = ATTN_SCALE * math.log2(math.e)
A_GROUP = 4
C_VDIM = 2 * HEAD_DIM

F32 = jnp.float32
BF16 = jnp.bfloat16

VMEM_LIMIT_BYTES = 56 * 1024 * 1024
IN_PROJ_VMEM_LIMIT_BYTES = 61 * 1024 * 1024
ROW_TILE = 256
KV_CHUNK = 256
A_LAT_CHUNK = 1024
C_LAT_CHUNK = 256
PROJ_TN = 512
IN_PROJ_TN = 1024
PROJ_TM = 512
OUT_PROJ_LATENT_TM = 1024
PROJ_SUBTILES = 4
ATTN_SUBTILES = 2
ATTN_STEP_TILES = 16


def _cparams(sem, vmem_limit_bytes=VMEM_LIMIT_BYTES):
    return pltpu.CompilerParams(dimension_semantics=sem, vmem_limit_bytes=vmem_limit_bytes)


def _silu(x):
    return x * (1.0 / (1.0 + jnp.exp(-x)))


def _mod_kernel(c_ref, w_ref, b_ref, o_ref):
    s = _silu(c_ref[...]).astype(BF16)
    w = w_ref[0].astype(BF16)
    o_ref[0] = jnp.dot(s, w, preferred_element_type=F32) + b_ref[0]


def _modulation(cvecs, w_mod, b_mod):
    depth, d, n3 = w_mod.shape
    tn = PROJ_TN
    return pl.pallas_call(
        _mod_kernel,
        grid=(depth, n3 // tn),
        in_specs=[
            pl.BlockSpec((8, d), lambda l, j: (0, 0)),
            pl.BlockSpec((1, d, tn), lambda l, j: (l, 0, j)),
            pl.BlockSpec((1, 1, tn), lambda l, j: (l, 0, j)),
        ],
        out_specs=pl.BlockSpec((1, 8, tn), lambda l, j: (l, 0, j)),
        out_shape=jax.ShapeDtypeStruct((depth, 8, n3), F32),
        compiler_params=_cparams(("arbitrary", "arbitrary")),
        name="modulation",
    )(cvecs, w_mod, b_mod.reshape(depth, 1, n3))


def _norm_math(h, g, scale, shift):
    ms = jnp.mean(h * h, axis=-1, keepdims=True)
    return ((h * lax.rsqrt(ms + EPS)) * g * (1.0 + scale) + shift).astype(BF16)


def _norm_kernel(lat_tiles, lat_ref, ctx_ref, g_ref, sc_ref, sh_ref, n_ref):
    h = jnp.where(pl.program_id(0) >= lat_tiles, ctx_ref[...], lat_ref[...])
    n_ref[...] = _norm_math(h, g_ref[...], sc_ref[0], sh_ref[0])


def _mod_row(t, lat_tiles, tiles_per_batch, ctx_row):
    return jnp.where(t >= lat_tiles, ctx_row, t // tiles_per_batch)


def _adaln_norm(h_lat, h_ctx, g, scale, shift, batch):
    d = h_lat.shape[1]
    lat_tiles = h_lat.shape[0] // ROW_TILE
    n_tiles = lat_tiles + h_ctx.shape[0] // ROW_TILE
    mrow = lambda t: (_mod_row(t, lat_tiles, lat_tiles // batch, batch), 0, 0)
    tile = lambda imap: pl.BlockSpec((ROW_TILE, d), imap)
    return pl.pallas_call(
        functools.partial(_norm_kernel, lat_tiles),
        grid=(n_tiles,),
        in_specs=[tile(lambda t: (jnp.minimum(t, lat_tiles - 1), 0)),
                  tile(lambda t: (jnp.maximum(t - lat_tiles, 0), 0)),
                  pl.BlockSpec((1, d), lambda t: (0, 0)),
                  pl.BlockSpec((1, 1, d), mrow), pl.BlockSpec((1, 1, d), mrow)],
        out_specs=tile(lambda t: (t, 0)),
        out_shape=jax.ShapeDtypeStruct((n_tiles * ROW_TILE, d), BF16),
        compiler_params=_cparams(("arbitrary",)),
        name="adaln_norm",
    )(h_lat, h_ctx, g, scale, shift)


def _rope_partner(x):
    lane = lax.broadcasted_iota(jnp.int32, x.shape, 1)
    fwd = pltpu.roll(x, HEAD_DIM - AXIS_DIM // 2, axis=1)
    bwd = pltpu.roll(x, AXIS_DIM // 2, axis=1)
    return jnp.where((lane % AXIS_DIM) < AXIS_DIM // 2, fwd, bwd)


def _proj_kernel(kinds, q_tile0, ctx_skip, n_ref, w_ref, cos_ref, sin_ref, qg_ref, kg_ref, o_ref, wbf_ref):
    j = pl.program_id(0)

    @pl.when(pl.program_id(1) == 0)
    def _():
        wbf_ref[...] = w_ref[...].astype(BF16)

    tm, tn = o_ref.shape
    is_q = j >= q_tile0
    qscale = jnp.where(is_q, Q_PRESCALE, 1.0).astype(F32)

    def rotary(x, cs, ss):
        return x * cs + _rope_partner(x) * ss

    def epilogue(pattern, acc, rows):
        if all(kind == "plain" for kind in pattern):
            o_ref[rows, :] = acc.astype(BF16)
            return
        g = jnp.where(is_q, qg_ref[...], kg_ref[...])
        cs = cos_ref[rows, :] * qscale
        ss = sin_ref[rows, :] * qscale
        xs = [acc[:, hd * HEAD_DIM:(hd + 1) * HEAD_DIM] for hd in range(len(pattern))]
        rs = [lax.rsqrt(jnp.mean(x * x, axis=-1, keepdims=True) + EPS) if kind == "norm" else None
              for x, kind in zip(xs, pattern)]
        for hd, kind in enumerate(pattern):
            if kind == "norm":
                y = rotary(xs[hd] * g, cs, ss) * rs[hd]
            elif kind == "rope":
                y = rotary(xs[hd], cs, ss)
            else:
                y = xs[hd]
            o_ref[rows, hd * HEAD_DIM:(hd + 1) * HEAD_DIM] = y.astype(BF16)

    def tiles(pattern):
        sub = tm // PROJ_SUBTILES
        spans = [pl.ds(r * sub, sub) for r in range(PROJ_SUBTILES)]
        dot = lambda rows: jnp.dot(n_ref[rows, :], wbf_ref[...], preferred_element_type=F32)
        acc_next = dot(spans[0])
        for r, rows in enumerate(spans):
            acc = acc_next
            if r + 1 < len(spans):
                acc_next = dot(spans[r + 1])
            epilogue(pattern, acc, rows)

    needed = True
    if ctx_skip is not None:
        needed = (pl.program_id(1) < ctx_skip[0]) | (j < ctx_skip[1])

        @pl.when(jnp.logical_not(needed))
        def _():
            o_ref[...] = jnp.zeros(o_ref.shape, BF16)

    for pattern in sorted(set(kinds)):
        cond = functools.reduce(jnp.logical_or, [j == jj for jj, p in enumerate(kinds) if p == pattern])
        pl.when(cond & needed)(functools.partial(tiles, pattern))


def _in_projection(n, w_in, layer, cos, sin, qg, kg, head_kinds, q_col0, tm, ctx_skip_rows):
    rows, d = n.shape
    cols = w_in.shape[2]
    tn = IN_PROJ_TN
    hpt = tn // HEAD_DIM
    kinds = tuple(tuple(head_kinds[j * hpt:(j + 1) * hpt]) for j in range(cols // tn))
    return pl.pallas_call(
        functools.partial(_proj_kernel, kinds, q_col0 // tn,
                          None if ctx_skip_rows is None else (ctx_skip_rows // tm, q_col0 // tn)),
        grid=(cols // tn, rows // tm),
        in_specs=[
            pl.BlockSpec((tm, d), lambda j, i: (i, 0)),
            pl.BlockSpec((None, d, tn), lambda j, i: (layer, 0, j)),
            pl.BlockSpec((tm, HEAD_DIM), lambda j, i: (i, 0)),
            pl.BlockSpec((tm, HEAD_DIM), lambda j, i: (i, 0)),
            pl.BlockSpec((1, HEAD_DIM), lambda j, i: (0, 0)),
            pl.BlockSpec((1, HEAD_DIM), lambda j, i: (0, 0)),
        ],
        out_specs=pl.BlockSpec((tm, tn), lambda j, i: (i, j)),
        out_shape=jax.ShapeDtypeStruct((rows, cols), BF16),
        scratch_shapes=[pltpu.VMEM((d, tn), BF16)],
        compiler_params=_cparams(("arbitrary", "arbitrary"), IN_PROJ_VMEM_LIMIT_BYTES),
        name="in_projection",
    )(n, w_in, cos, sin, qg, kg)


_NT = (((1,), (1,)), ((), ()))
_TN = (((0,), (0,)), ((), ()))


def _softmax_pipeline(items, m_ref, l_ref, acc_ref):
    st_next = items[0][0]()
    for i, (_, value, lanes, first, done) in enumerate(items):
        st = st_next
        if i + 1 < len(items):
            st_next = items[i + 1][0]()
        m_new = jnp.max(st, axis=0, keepdims=True)
        if not first:
            m_old = m_ref[:, lanes]
            m_new = jnp.maximum(m_old, m_new)
            alpha = jnp.exp2(m_old - m_new)
        p = jnp.exp2(st - m_new)
        l_new = jnp.sum(p, axis=0, keepdims=True)
        acc_new = lax.dot_general(value(), p.astype(BF16), _TN, preferred_element_type=F32)
        if not first:
            l_new += alpha * l_ref[:, lanes]
            acc_new += alpha * acc_ref[:, lanes]
        m_ref[:, lanes] = m_new
        l_ref[:, lanes] = l_new
        acc_ref[:, lanes] = acc_new
        if done is not None:
            done()


def _key_chunks(kc_ref, kl_ref, lat_chunk):
    chunks = [(kc_ref, 0, kc_ref.shape[0])]
    if kl_ref is not None:
        chunks += [(kl_ref, off, lat_chunk) for off in range(0, kl_ref.shape[0], lat_chunk)]
    return chunks


def _for_subtile_groups(n_tiles, run):
    n_sub = min(ATTN_SUBTILES, n_tiles)
    if n_tiles == n_sub:
        run(0, n_sub)
        return
    group_rows = n_sub * ROW_TILE

    def body(it, carry):
        run(pl.multiple_of(it * group_rows, group_rows), n_sub)
        return carry

    lax.fori_loop(0, n_tiles // n_sub, body, 0)


def _attn_a_kernel(lat_chunk, latent, *refs):
    if latent:
        q_ref, kc_ref, kl_ref, vc_ref, vl_ref, z_ref, o_ref, m_ref, l_ref, acc_ref = refs
    else:
        q_ref, kc_ref, vc_ref, z_ref, o_ref, m_ref, l_ref, acc_ref = refs
        kl_ref = vl_ref = None
    tq = ROW_TILE
    lanes_per_tile = A_GROUP * tq
    k_chunks = _key_chunks(kc_ref, kl_ref, lat_chunk)
    v_chunks = _key_chunks(vc_ref, vl_ref, lat_chunk)

    def run(row0, n_sub):
        items = []
        for s in range(n_sub):
            rows = pl.ds(row0 + s * tq, tq)
            lanes = slice(s * lanes_per_tile, (s + 1) * lanes_per_tile)
            q4 = jnp.concatenate([q_ref[rows, g * HEAD_DIM:(g + 1) * HEAD_DIM] for g in range(A_GROUP)], axis=0)

            def finalize(rows=rows, lanes=lanes):
                o = jnp.transpose(acc_ref[:, lanes] * (1.0 / l_ref[:, lanes]))
                z = z_ref[rows, :].astype(F32)
                for g in range(A_GROUP):
                    og = o[g * tq:(g + 1) * tq] * _silu(z[:, g * HEAD_DIM:(g + 1) * HEAD_DIM])
                    o_ref[rows, g * HEAD_DIM:(g + 1) * HEAD_DIM] = og.astype(BF16)

            for c, ((k_ref, off, size), (v_ref, _, _)) in enumerate(zip(k_chunks, v_chunks)):
                score = lambda k_ref=k_ref, off=off, size=size, q4=q4: lax.dot_general(
                    k_ref[pl.ds(off, size), :], q4, _NT, preferred_element_type=F32)
                value = lambda v_ref=v_ref, off=off, size=size: v_ref[pl.ds(off, size), :]
                items.append((score, value, lanes, c == 0, finalize if c == len(k_chunks) - 1 else None))
        _softmax_pipeline(items, m_ref, l_ref, acc_ref)

    _for_subtile_groups(q_ref.shape[0] // tq, run)


def _attention_a(u, batch, seq, ctx_len, with_ctx, cols):
    ka0, va0, qa0, za0, a_width = cols
    kv_heads = a_width // (A_GROUP * HEAD_DIM)
    qw = A_GROUP * HEAD_DIM
    tq = min(ATTN_STEP_TILES * ROW_TILE, seq)
    lat_steps = seq // tq
    lat_rows = batch * seq
    ctx_row = lat_rows // ctx_len
    ctx_kv = lambda c0: pl.BlockSpec((ctx_len, HEAD_DIM), lambda b, h, *t: (ctx_row + b, c0 // HEAD_DIM + h))
    lat_kv = lambda c0: pl.BlockSpec((seq, HEAD_DIM), lambda b, h, t: (b, c0 // HEAD_DIM + h))
    scratch = lambda tiles: [pltpu.VMEM((1, tiles * A_GROUP * ROW_TILE), F32),
                             pltpu.VMEM((1, tiles * A_GROUP * ROW_TILE), F32),
                             pltpu.VMEM((HEAD_DIM, tiles * A_GROUP * ROW_TILE), F32)]
    oa = pl.pallas_call(
        functools.partial(_attn_a_kernel, A_LAT_CHUNK, True),
        grid=(batch, kv_heads, lat_steps),
        in_specs=[
            pl.BlockSpec((tq, qw), lambda b, h, t: (b * lat_steps + t, qa0 // qw + h)),
            ctx_kv(ka0), lat_kv(ka0), ctx_kv(va0), lat_kv(va0),
            pl.BlockSpec((tq, qw), lambda b, h, t: (b * lat_steps + t, za0 // qw + h)),
        ],
        out_specs=pl.BlockSpec((tq, qw), lambda b, h, t: (b * lat_steps + t, h)),
        out_shape=jax.ShapeDtypeStruct((lat_rows, a_width), BF16),
        scratch_shapes=scratch(ATTN_SUBTILES),
        compiler_params=_cparams(("arbitrary", "arbitrary", "arbitrary")),
        name="attention_a",
    )(u, u, u, u, u, u)
    if not with_ctx:
        return oa, None
    oa_ctx = pl.pallas_call(
        functools.partial(_attn_a_kernel, A_LAT_CHUNK, False),
        grid=(batch, kv_heads),
        in_specs=[
            pl.BlockSpec((ctx_len, qw), lambda b, h: (ctx_row + b, qa0 // qw + h)),
            ctx_kv(ka0), ctx_kv(va0),
            pl.BlockSpec((ctx_len, qw), lambda b, h: (ctx_row + b, za0 // qw + h)),
        ],
        out_specs=pl.BlockSpec((ctx_len, qw), lambda b, h: (b, h)),
        out_shape=jax.ShapeDtypeStruct((batch * ctx_len, a_width), BF16),
        scratch_shapes=scratch(1),
        compiler_params=_cparams(("arbitrary", "arbitrary")),
        name="attention_a_ctx",
    )(u, u, u, u)
    return oa, oa_ctx


def _attn_c_kernel(lat_chunk, latent, one_minus_lambda_init, lambda_init, *refs):
    if latent:
        (q_ref, kc_ref, kl_ref, vc_ref, vl_ref, z_ref, lq1_ref, lk1_ref, lq2_ref, lk2_ref, sg_ref,
         o_ref, m_ref, l_ref, acc_ref) = refs
    else:
        (q_ref, kc_ref, vc_ref, z_ref, lq1_ref, lk1_ref, lq2_ref, lk2_ref, sg_ref,
         o_ref, m_ref, l_ref, acc_ref) = refs
        kl_ref = vl_ref = None
    tq = ROW_TILE
    lam = (jnp.exp(jnp.sum(lq1_ref[...] * lk1_ref[...], axis=-1, keepdims=True)) -
           jnp.exp(jnp.sum(lq2_ref[...] * lk2_ref[...], axis=-1, keepdims=True)) + lambda_init)
    k_chunks = _key_chunks(kc_ref, kl_ref, lat_chunk)
    v_chunks = _key_chunks(vc_ref, vl_ref, lat_chunk)

    def run(row0, n_sub):
        items = []
        for s in range(n_sub):
            rows = pl.ds(row0 + s * tq, tq)
            lane0 = 2 * s * tq

            def finalize(rows=rows, lane0=lane0):
                lanes = slice(lane0, lane0 + 2 * tq)
                ot = acc_ref[:, lanes] * (1.0 / l_ref[:, lanes])
                od = jnp.transpose(ot[:, :tq] - lam * ot[:, tq:])
                y = od * lax.rsqrt(jnp.mean(od * od, axis=-1, keepdims=True) + EPS) * sg_ref[...]
                y = y * one_minus_lambda_init * _silu(z_ref[rows, :].astype(F32))
                o_ref[rows, :] = y.astype(BF16)

            for c, ((k_ref, off, size), (v_ref, _, _)) in enumerate(zip(k_chunks, v_chunks)):
                for g in range(2):
                    lo = g * HEAD_DIM
                    qc = q_ref[rows, lo:lo + HEAD_DIM]
                    score = lambda k_ref=k_ref, off=off, size=size, lo=lo, qc=qc: lax.dot_general(
                        k_ref[pl.ds(off, size), lo:lo + HEAD_DIM], qc, _NT, preferred_element_type=F32)
                    value = lambda v_ref=v_ref, off=off, size=size: v_ref[pl.ds(off, size), :]
                    last = c == len(k_chunks) - 1 and g == 1
                    items.append((score, value, slice(lane0 + g * tq, lane0 + (g + 1) * tq), c == 0,
                                  finalize if last else None))
        _softmax_pipeline(items, m_ref, l_ref, acc_ref)

    _for_subtile_groups(q_ref.shape[0] // tq, run)


def _attention_c(u, lam_vecs, subln_g, lambda_init, batch, seq, ctx_len, with_ctx, cols):
    kc0, vc0, qc0, zc0, c_width = cols
    heads = c_width // C_VDIM
    tq = min(ATTN_STEP_TILES * ROW_TILE, seq)
    lat_steps = seq // tq
    lat_rows = batch * seq
    ctx_row = lat_rows // ctx_len
    vec = lambda n: pl.BlockSpec((1, n), lambda b, h, *t: (0, 0))
    params = [vec(HEAD_DIM)] * 4 + [vec(C_VDIM)]
    ctx_kv = lambda c0: pl.BlockSpec((ctx_len, C_VDIM), lambda b, h, *t: (ctx_row + b, c0 // C_VDIM + h))
    lat_kv = lambda c0: pl.BlockSpec((seq, C_VDIM), lambda b, h, t: (b, c0 // C_VDIM + h))
    scratch = lambda tiles: [pltpu.VMEM((1, tiles * 2 * ROW_TILE), F32), pltpu.VMEM((1, tiles * 2 * ROW_TILE), F32),
                             pltpu.VMEM((C_VDIM, tiles * 2 * ROW_TILE), F32)]
    oc = pl.pallas_call(
        functools.partial(_attn_c_kernel, C_LAT_CHUNK, True, 1.0 - lambda_init, lambda_init),
        grid=(batch, heads, lat_steps),
        in_specs=[
            pl.BlockSpec((tq, C_VDIM), lambda b, h, t: (b * lat_steps + t, qc0 // C_VDIM + h)),
            ctx_kv(kc0), lat_kv(kc0), ctx_kv(vc0), lat_kv(vc0),
            pl.BlockSpec((tq, C_VDIM), lambda b, h, t: (b * lat_steps + t, zc0 // C_VDIM + h)),
        ] + params,
        out_specs=pl.BlockSpec((tq, C_VDIM), lambda b, h, t: (b * lat_steps + t, h)),
        out_shape=jax.ShapeDtypeStruct((lat_rows, c_width), BF16),
        scratch_shapes=scratch(ATTN_SUBTILES),
        compiler_params=_cparams(("arbitrary", "arbitrary", "arbitrary")),
        name="attention_c",
    )(u, u, u, u, u, u, *lam_vecs, subln_g)
    if not with_ctx:
        return oc, None
    oc_ctx = pl.pallas_call(
        functools.partial(_attn_c_kernel, C_LAT_CHUNK, False, 1.0 - lambda_init, lambda_init),
        grid=(batch, heads),
        in_specs=[
            pl.BlockSpec((ctx_len, C_VDIM), lambda b, h: (ctx_row + b, qc0 // C_VDIM + h)),
            ctx_kv(kc0), ctx_kv(vc0),
            pl.BlockSpec((ctx_len, C_VDIM), lambda b, h: (ctx_row + b, zc0 // C_VDIM + h)),
        ] + params,
        out_specs=pl.BlockSpec((ctx_len, C_VDIM), lambda b, h: (b, h)),
        out_shape=jax.ShapeDtypeStruct((batch * ctx_len, c_width), BF16),
        scratch_shapes=scratch(1),
        compiler_params=_cparams(("arbitrary", "arbitrary")),
        name="attention_c_ctx",
    )(u, u, u, u, *lam_vecs, subln_g)
    return oc, oc_ctx


HALO = 16


def _conv_kernel(lat_tiles, tiles_per_batch, xb_ref, bb_ref, cb_ref, zb_ref, xp_ref, cp_ref, xn_ref, cn_ref,
                 w_ref, o_ref, g_ref):
    tr = xb_ref.shape[0]
    t = pl.program_id(0)
    r = t % tiles_per_batch
    has_prev = ((t < lat_tiles) & (r != 0)).astype(F32)
    has_next = ((t < lat_tiles) & (r != tiles_per_batch - 1)).astype(F32)
    g = cb_ref[...].astype(F32) * xb_ref[...].astype(F32)
    g_ref[0:HALO, :] = cp_ref[...].astype(F32) * xp_ref[...].astype(F32) * has_prev
    g_ref[HALO:HALO + tr, :] = g
    g_ref[HALO + tr:HALO + tr + HALO, :] = cn_ref[...].astype(F32) * xn_ref[...].astype(F32) * has_next
    w = w_ref[...]
    conv = (g_ref[HALO - 1:HALO - 1 + tr, :] * w[0:1] + g * w[1:2] +
            g_ref[HALO + 1:HALO + 1 + tr, :] * w[2:3])
    y = bb_ref[...].astype(F32) * conv * _silu(zb_ref[...].astype(F32))
    o_ref[...] = y.astype(BF16)


def _short_conv(u, conv_w, batch, seq, with_ctx, cols):
    rows = u.shape[0]
    xb0, bb0, cb0, zb0, bw = cols
    tr = ROW_TILE
    lat_tiles = batch * seq // tr
    n_tiles = rows // tr if with_ctx else lat_tiles
    hpt = tr // HALO
    last_halo = rows // HALO - 1
    main = lambda c0: pl.BlockSpec((tr, bw), lambda t: (t, c0 // bw))
    prev = lambda c0: pl.BlockSpec((HALO, bw), lambda t: (jnp.maximum(t * hpt - 1, 0), c0 // bw))
    nxt = lambda c0: pl.BlockSpec((HALO, bw), lambda t: (jnp.minimum((t + 1) * hpt, last_halo), c0 // bw))
    return pl.pallas_call(
        functools.partial(_conv_kernel, lat_tiles, seq // tr),
        grid=(n_tiles,),
        in_specs=[main(xb0), main(bb0), main(cb0), main(zb0),
                  prev(xb0), prev(cb0), nxt(xb0), nxt(cb0),
                  pl.BlockSpec(conv_w.shape, lambda t: (0, 0))],
        out_specs=pl.BlockSpec((tr, bw), lambda t: (t, 0)),
        out_shape=jax.ShapeDtypeStruct((n_tiles * tr, bw), BF16),
        scratch_shapes=[pltpu.VMEM((tr + 2 * HALO, bw), F32)],
        compiler_params=_cparams(("arbitrary",)),
        name="short_conv",
    )(u, u, u, u, u, u, u, u, conv_w)


def _out_kernel(oa_ref, yb_ref, oc_ref, res_ref, w_ref, gate_ref, o_ref, wbf_ref):
    @pl.when(pl.program_id(1) == 0)
    def _():
        wbf_ref[...] = w_ref[...].astype(BF16)

    ka = oa_ref.shape[1]
    kb = yb_ref.shape[1]
    sub = o_ref.shape[0] // PROJ_SUBTILES
    gate = gate_ref[0]

    def dot(rows):
        acc = jnp.dot(oa_ref[rows, :], wbf_ref[0:ka, :], preferred_element_type=F32)
        acc += jnp.dot(yb_ref[rows, :], wbf_ref[ka:ka + kb, :], preferred_element_type=F32)
        return acc + jnp.dot(oc_ref[rows, :], wbf_ref[ka + kb:, :], preferred_element_type=F32)

    spans = [pl.ds(r * sub, sub) for r in range(PROJ_SUBTILES)]
    acc_next = dot(spans[0])
    for r, rows in enumerate(spans):
        acc = acc_next
        if r + 1 < len(spans):
            acc_next = dot(spans[r + 1])
        o_ref[rows, :] = res_ref[rows, :] + gate * acc


def _out_projection(oa, yb, oc, res, w_out, layer, gates, tm, yb_row0, gate_row):
    rows = oa.shape[0]
    d = w_out.shape[2]
    tn = PROJ_TN
    kdim = w_out.shape[1]
    yb0 = yb_row0 // tm
    return pl.pallas_call(
        _out_kernel,
        grid=(d // tn, rows // tm),
        in_specs=[
            pl.BlockSpec((tm, oa.shape[1]), lambda j, i: (i, 0)),
            pl.BlockSpec((tm, yb.shape[1]), lambda j, i: (yb0 + i, 0)),
            pl.BlockSpec((tm, oc.shape[1]), lambda j, i: (i, 0)),
            pl.BlockSpec((tm, tn), lambda j, i: (i, j)),
            pl.BlockSpec((None, kdim, tn), lambda j, i: (layer, 0, j)),
            pl.BlockSpec((1, 1, tn), lambda j, i: (gate_row(i), 0, j)),
        ],
        out_specs=pl.BlockSpec((tm, tn), lambda j, i: (i, j)),
        out_shape=jax.ShapeDtypeStruct((rows, d), F32),
        scratch_shapes=[pltpu.VMEM((kdim, tn), BF16)],
        compiler_params=_cparams(("arbitrary", "arbitrary")),
        name="out_projection",
    )(oa, yb, oc, res, w_out, gates)


def _final_kernel(h_ref, g_ref, o_ref):
    h = h_ref[...]
    ms = jnp.mean(h * h, axis=-1, keepdims=True)
    o_ref[...] = (h * lax.rsqrt(ms + EPS)) * g_ref[...]


def _final_norm(h, g):
    rows, d = h.shape
    return pl.pallas_call(
        _final_kernel,
        grid=(rows // ROW_TILE,),
        in_specs=[pl.BlockSpec((ROW_TILE, d), lambda t: (t, 0)), pl.BlockSpec((1, d), lambda t: (0, 0))],
        out_specs=pl.BlockSpec((ROW_TILE, d), lambda t: (t, 0)),
        out_shape=jax.ShapeDtypeStruct((rows, d), F32),
        compiler_params=_cparams(("arbitrary",)),
        name="final_norm",
    )(h, g)


def _rope_tables(batch, seq, ctx_len):
    pos = jnp.arange(seq)
    inv = ROPE_THETA ** (-jnp.arange(0, AXIS_DIM, 2, dtype=F32) / AXIS_DIM)
    ar = (pos // GRID_W).astype(F32)[:, None] * inv
    ac = (pos % GRID_W).astype(F32)[:, None] * inv
    cos = jnp.concatenate([jnp.cos(ar), jnp.cos(ar), jnp.cos(ac), jnp.cos(ac)], axis=-1)
    sin = jnp.concatenate([-jnp.sin(ar), jnp.sin(ar), -jnp.sin(ac), jnp.sin(ac)], axis=-1)
    cos = jnp.concatenate([jnp.tile(cos, (batch, 1)), jnp.ones((batch * ctx_len, HEAD_DIM), F32)], axis=0)
    sin = jnp.concatenate([jnp.tile(sin, (batch, 1)), jnp.zeros((batch * ctx_len, HEAD_DIM), F32)], axis=0)
    return cos, sin


def kernel(x, c, ctx, c_ctx, w_mod, b_mod, norm_g, w_in, q_norm_a, k_norm_a, conv_w, lambda_q1,
           lambda_k1, lambda_q2, lambda_k2, subln_g, w_out, final_g):
    batch, seq, d = x.shape
    ctx_len = ctx.shape[1]
    depth = w_in.shape[0]
    lat_rows, ctx_rows = batch * seq, batch * ctx_len
    assert ctx_len == ROW_TILE == KV_CHUNK and seq % ROW_TILE == 0
    assert lat_rows % PROJ_TM == 0 and ctx_rows % PROJ_TM == 0 and seq % OUT_PROJ_LATENT_TM == 0

    a_width, b_width, c_width = d // 2, d // 4, d // 4
    kv_a = a_width // A_GROUP
    ka0 = 0
    va0 = ka0 + kv_a
    kc0 = va0 + kv_a
    vc0 = kc0 + c_width
    qa0 = vc0 + c_width
    qc0 = qa0 + a_width
    xb0 = qc0 + c_width
    bb0 = xb0 + b_width
    cb0 = bb0 + b_width
    za0 = cb0 + b_width
    zb0 = za0 + a_width
    zc0 = zb0 + b_width
    assert zc0 + c_width == w_in.shape[2]
    head_kind = lambda c: ("norm" if c < va0 or qa0 <= c < qc0 else
                           "rope" if kc0 <= c < vc0 or qc0 <= c < xb0 else "plain")
    head_kinds = [head_kind(c) for c in range(0, w_in.shape[2], HEAD_DIM)]
    assert qa0 % IN_PROJ_TN == 0

    cvecs = jnp.concatenate([c, c_ctx[None, :], jnp.zeros((8 - batch - 1, d), F32)], axis=0)
    mods = _modulation(cvecs, w_mod, b_mod)
    cos, sin = _rope_tables(batch, seq, ctx_len)
    x2 = x.reshape(lat_rows, d)
    c2 = ctx.reshape(ctx_rows, d)

    h_lat, h_ctx = x2, c2
    lat_tm, ctx_tm = OUT_PROJ_LATENT_TM, PROJ_TM
    for i in range(depth):
        update_ctx = i < depth - 1
        lambda_init = 0.8 - 0.6 * math.exp(-0.3 * i)
        shift = mods[i, :, 0:d].reshape(8, 1, d)
        scale = mods[i, :, d:2 * d].reshape(8, 1, d)
        gates = mods[i, :, 2 * d:3 * d].reshape(8, 1, d)
        n = _adaln_norm(h_lat, h_ctx, norm_g[i].reshape(1, d), scale, shift, batch)
        u = _in_projection(n, w_in, i, cos, sin, q_norm_a[i].reshape(1, HEAD_DIM),
                           k_norm_a[i].reshape(1, HEAD_DIM), head_kinds, qa0, PROJ_TM,
                           None if update_ctx else lat_rows)
        oa, oa_ctx = _attention_a(u, batch, seq, ctx_len, update_ctx, (ka0, va0, qa0, za0, a_width))
        lam_vecs = [v[i].reshape(1, HEAD_DIM) for v in (lambda_q1, lambda_k1, lambda_q2, lambda_k2)]
        oc, oc_ctx = _attention_c(u, lam_vecs, subln_g[i].reshape(1, C_VDIM), lambda_init, batch, seq, ctx_len,
                                  update_ctx, (kc0, vc0, qc0, zc0, c_width))
        yb = _short_conv(u, conv_w[i], batch, seq, update_ctx, (xb0, bb0, cb0, zb0, b_width))
        if update_ctx:
            h_ctx = _out_projection(oa_ctx, yb, oc_ctx, h_ctx, w_out, i, gates, ctx_tm, lat_rows,
                                    lambda t: batch)
        h_lat = _out_projection(oa, yb, oc, h_lat, w_out, i, gates, lat_tm, 0,
                                lambda t: t // (seq // lat_tm))
    return _final_norm(h_lat, final_g.reshape(1, d)).reshape(batch, seq, d)
```
